```python
import jax, jax.numpy as jnp
from jax import lax
import numpy as np

D_MODEL = 1024
BATCH = 32
SEQ = 2048
DEPTH = 4

GRID_W = 64
CTX_LEN = 256
CONV_DIM = 1024
CONV_WIDTH = 3
MLSTM_HEADS = 4
MLSTM_HEAD_DIM = 256
MLSTM_DIM = MLSTM_HEADS * MLSTM_HEAD_DIM
MLSTM_CHUNK = 64
MLSTM_K_SCALE = MLSTM_HEAD_DIM ** -0.5
NEG_INIT = -1e30
MLA_HEADS = 16
QK_NOPE_DIM = 128
QK_ROPE_DIM = 64
V_HEAD_DIM = 128
Q_LORA_RANK = 384
KV_LORA_RANK = 256
MLA_DIM = MLA_HEADS * V_HEAD_DIM
MLA_SCALE = (QK_NOPE_DIM + QK_ROPE_DIM) ** -0.5
ROPE_BASE = 10000.0
Q_BLOCK = 128
EVEN_IN_SIZES = (CONV_DIM, CONV_DIM, CONV_DIM, CONV_DIM, MLSTM_DIM, MLSTM_DIM, MLSTM_DIM, MLSTM_DIM, MLSTM_DIM, 4 * MLSTM_HEADS)
EVEN_IN_DIM = 4 * CONV_DIM + 5 * MLSTM_DIM + 4 * MLSTM_HEADS
EVEN_MIX_DIM = CONV_DIM + MLSTM_DIM
ODD_IN_SIZES = (Q_LORA_RANK, KV_LORA_RANK, QK_ROPE_DIM, MLA_DIM)
ODD_IN_DIM = Q_LORA_RANK + KV_LORA_RANK + QK_ROPE_DIM + MLA_DIM
N_EVEN = (DEPTH + 1) // 2
N_ODD = DEPTH // 2
LN_EPS = 1e-5
RMS_EPS = 1e-6
DEEPNORM_ALPHA = (2 * DEPTH) ** 0.25
DEEPNORM_BETA = (8 * DEPTH) ** -0.25

kernel_name = 'hybrid_conv_mlstm_mla_prefix_trunk'


def layer_norm(x, g, b):
    xf = x.astype(jnp.float32)
    mu = jnp.mean(xf, axis=-1, keepdims=True)
    var = jnp.mean(jnp.square(xf - mu), axis=-1, keepdims=True)
    return ((xf - mu) * lax.rsqrt(var + LN_EPS) * g + b).astype(x.dtype)


def rms_norm(x, g):
    xf = x.astype(jnp.float32)
    return (xf * lax.rsqrt(jnp.mean(jnp.square(xf), axis=-1, keepdims=True) + RMS_EPS) * g).astype(x.dtype)


def split_cols(p, sizes):
    out, start = [], 0
    for s in sizes:
        out.append(p[..., start:start + s])
        start += s
    return out


def axial_rope_tables(n_lat):
    rows = n_lat // GRID_W
    row = jnp.repeat(jnp.arange(rows, dtype=jnp.float32), GRID_W)
    col = jnp.tile(jnp.arange(GRID_W, dtype=jnp.float32), rows)
    n_freq = QK_ROPE_DIM // 4
    inv_freq = ROPE_BASE ** (-jnp.arange(n_freq, dtype=jnp.float32) / n_freq)
    ang = jnp.concatenate([row[:, None] * inv_freq, col[:, None] * inv_freq], axis=-1)
    return jnp.cos(ang), jnp.sin(ang)


def apply_rope(x, cos, sin):
    half = QK_ROPE_DIM // 2
    x1, x2 = x[..., :half], x[..., half:]
    return jnp.concatenate([x1 * cos - x2 * sin, x1 * sin + x2 * cos], axis=-1).astype(x.dtype)


def short_conv3(u, w):
    up = jnp.pad(u, ((0, 0), (1, 1), (0, 0)))
    return up[:, :-2] * w[0] + up[:, 1:-1] * w[1] + up[:, 2:] * w[2]


def mlstm_zero_state(bsz):
    return (jnp.zeros((bsz, MLSTM_HEADS, MLSTM_HEAD_DIM, MLSTM_HEAD_DIM), jnp.float32),
            jnp.zeros((bsz, MLSTM_HEADS, MLSTM_HEAD_DIM), jnp.float32),
            jnp.full((bsz, MLSTM_HEADS), NEG_INIT, jnp.float32))


def mlstm_chunkwise(q, k, v, i_pre, log_f, state):
    bsz, nh, t_len, dh = q.shape
    n_chunk = t_len // MLSTM_CHUNK

    def chunks(a):
        a = a.astype(jnp.float32).reshape((bsz, nh, n_chunk, MLSTM_CHUNK) + a.shape[3:])
        return jnp.moveaxis(a, 2, 0)

    xs = (chunks(q), chunks(k), chunks(v), chunks(i_pre), chunks(log_f))
    lower = jnp.tril(jnp.ones((MLSTM_CHUNK, MLSTM_CHUNK), dtype=bool))

    def step(carry, inp):
        c_st, n_st, m_st = carry
        qc, kc, vc, ic, fc = inp
        b = lax.cumsum(fc, axis=2)
        d = jnp.where(lower, b[..., :, None] - b[..., None, :] + ic[..., None, :], -jnp.inf)
        inter = b + m_st[..., None]
        m_t = jnp.maximum(inter, jnp.max(d, axis=-1))
        w = jnp.exp(d - m_t[..., None])
        a_inter = jnp.exp(inter - m_t)
        s = jnp.einsum('bhtd,bhsd->bhts', qc, kc) * w
        num = jnp.einsum('bhts,bhsd->bhtd', s, vc) + a_inter[..., None] * jnp.einsum('bhtk,bhkd->bhtd', qc, c_st)
        den = jnp.sum(s, axis=-1) + a_inter * jnp.einsum('bhtk,bhk->bht', qc, n_st)
        h = num / jnp.maximum(jnp.abs(den), jnp.exp(-m_t))[..., None]
        m_new = m_t[..., -1]
        decay = jnp.exp(b[..., -1] + m_st - m_new)
        w_tok = jnp.exp(b[..., -1:] - b + ic - m_new[..., None])
        kw = kc * w_tok[..., None]
        c_new = decay[..., None, None] * c_st + jnp.einsum('bhsk,bhsd->bhkd', kw, vc)
        n_new = decay[..., None] * n_st + jnp.sum(kw, axis=2)
        return (c_new, n_new, m_new), h

    state, hs = lax.scan(step, state, xs)
    return jnp.moveaxis(hs, 0, 2).reshape(bsz, nh, t_len, dh), state


def even_mixer(u_ctx, u_lat, w_in, conv_w, gate_b, w_out, need_ctx):
    s_ctx = split_cols(u_ctx @ w_in, EVEN_IN_SIZES)
    s_lat = split_cols(u_lat @ w_in, EVEN_IN_SIZES)

    def conv_branch(s):
        a_b, a_c, a_x, a_z = s[:4]
        return a_b * short_conv3(a_c * a_x, conv_w) * jax.nn.silu(a_z)

    def to_heads(a):
        bsz, t_len, _ = a.shape
        return a.reshape(bsz, t_len, MLSTM_HEADS, MLSTM_HEAD_DIM).transpose(0, 2, 1, 3)

    def mlstm_inputs(s):
        q, k, v, g = s[4], s[5], s[6], s[9]
        bsz, t_len, _ = g.shape
        g = (g + gate_b).astype(jnp.float32).reshape(bsz, t_len, 2, 2, MLSTM_HEADS)
        g = jnp.transpose(g, (2, 3, 0, 4, 1))
        return to_heads(q), to_heads(k) * MLSTM_K_SCALE, to_heads(v), g

    qc, kc, vc, gc = mlstm_inputs(s_ctx)
    ql, kl, vl, gl = mlstm_inputs(s_lat)
    h_ctx, h_lat = 0.0, 0.0
    for direction in range(2):
        f = (lambda a: jnp.flip(a, axis=2)) if direction == 1 else (lambda a: a)
        hc, st = mlstm_chunkwise(f(qc), f(kc), f(vc), f(gc[direction, 0]),
                                 jax.nn.log_sigmoid(f(gc[direction, 1])), mlstm_zero_state(qc.shape[0]))
        hl, _ = mlstm_chunkwise(f(ql), f(kl), f(vl), f(gl[direction, 0]),
                                jax.nn.log_sigmoid(f(gl[direction, 1])), st)
        h_lat = h_lat + f(hl)
        if need_ctx:
            h_ctx = h_ctx + f(hc)

    def mlstm_branch(h, s):
        o, z = s[7], s[8]
        bsz, t_len, _ = o.shape
        h = h.transpose(0, 2, 1, 3).reshape(bsz, t_len, MLSTM_DIM).astype(o.dtype)
        return jax.nn.sigmoid(o) * h * jax.nn.silu(z)

    def merge(s, h):
        return jnp.concatenate([conv_branch(s), mlstm_branch(h, s)], axis=-1) @ w_out

    y_lat = merge(s_lat, h_lat)
    y_ctx = merge(s_ctx, h_ctx) if need_ctx else None
    return y_ctx, y_lat


def mla_attend(q_n, q_r, k_n, k_r, v):
    s = jnp.einsum('bqhd,bkhd->bhqk', q_n, k_n) + jnp.einsum('bqhr,bkr->bhqk', q_r, k_r)
    p = jax.nn.softmax(s.astype(jnp.float32) * MLA_SCALE, axis=-1)
    return jnp.einsum('bhqk,bkhd->bqhd', p.astype(v.dtype), v)


def mla_mixer(u_ctx, u_lat, w_in, q_norm, w_uq, kv_norm, w_ukv, w_out, cos, sin, need_ctx):
    s_ctx = split_cols(u_ctx @ w_in, ODD_IN_SIZES)
    s_lat = split_cols(u_lat @ w_in, ODD_IN_SIZES)

    def kv_side(s):
        kv_c, k_r = s[1], s[2]
        bsz, t_len, _ = kv_c.shape
        kv = (rms_norm(kv_c, kv_norm) @ w_ukv).reshape(bsz, t_len, MLA_HEADS, QK_NOPE_DIM + V_HEAD_DIM)
        return kv[..., :QK_NOPE_DIM], k_r, kv[..., QK_NOPE_DIM:]

    def q_side(s):
        q_c = s[0]
        bsz, t_len, _ = q_c.shape
        q = (rms_norm(q_c, q_norm) @ w_uq).reshape(bsz, t_len, MLA_HEADS, QK_NOPE_DIM + QK_ROPE_DIM)
        return q[..., :QK_NOPE_DIM], q[..., QK_NOPE_DIM:]

    kn_c, kr_c, v_c = kv_side(s_ctx)
    kn_l, kr_l, v_l = kv_side(s_lat)
    kr_l = apply_rope(kr_l, cos, sin)
    qn_l, qr_l = q_side(s_lat)
    qr_l = apply_rope(qr_l, cos[:, None, :], sin[:, None, :])
    kn_all = jnp.concatenate([kn_c, kn_l], axis=1)
    kr_all = jnp.concatenate([kr_c, kr_l], axis=1)
    v_all = jnp.concatenate([v_c, v_l], axis=1)
    bsz, n_lat, _ = u_lat.shape
    n_blk = n_lat // Q_BLOCK

    def blocks(a):
        return jnp.moveaxis(a.reshape((bsz, n_blk, Q_BLOCK) + a.shape[2:]), 1, 0)

    o_blk = lax.map(lambda qs: mla_attend(qs[0], qs[1], kn_all, kr_all, v_all), (blocks(qn_l), blocks(qr_l)))
    o_lat = jnp.moveaxis(o_blk, 0, 1).reshape(bsz, n_lat, MLA_DIM)
    y_lat = (o_lat * jax.nn.silu(s_lat[3])) @ w_out
    y_ctx = None
    if need_ctx:
        qn_c, qr_c = q_side(s_ctx)
        o_ctx = mla_attend(qn_c, qr_c, kn_c, kr_c, v_c).reshape(bsz, -1, MLA_DIM)
        y_ctx = (o_ctx * jax.nn.silu(s_ctx[3])) @ w_out
    return y_ctx, y_lat


def setup_inputs(seed: int = 0) -> dict:
    key = jax.random.key(seed)
    ks = jax.random.split(key, 20)

    def nrm(k, shape, s):
        return jax.random.normal(k, shape, jnp.float32) * s

    i_bias = nrm(ks[10], (N_EVEN, 2, 1, MLSTM_HEADS), 0.1)
    f_bias = jnp.linspace(3.0, 6.0, MLSTM_HEADS, dtype=jnp.float32) + nrm(ks[11], (N_EVEN, 2, 1, MLSTM_HEADS), 0.1)
    return {
        'x': nrm(ks[0], (BATCH, SEQ, D_MODEL), 1.0),
        'c': nrm(ks[1], (BATCH, D_MODEL), 1.0),
        'ctx': nrm(ks[2], (BATCH, CTX_LEN, D_MODEL), 1.0),
        'c_ctx': nrm(ks[3], (D_MODEL,), 1.0),
        'mod_w': nrm(ks[4], (DEPTH, D_MODEL, 3 * D_MODEL), 0.5 * D_MODEL ** -0.5),
        'mod_b': nrm(ks[5], (DEPTH, 3 * D_MODEL), 0.01),
        'ln_g': 1.0 + nrm(ks[6], (DEPTH, D_MODEL), 0.01),
        'ln_b': nrm(ks[7], (DEPTH, D_MODEL), 0.01),
        'ev_w_in': nrm(ks[8], (N_EVEN, D_MODEL, EVEN_IN_DIM), D_MODEL ** -0.5),
        'ev_conv_w': nrm(ks[9], (N_EVEN, CONV_WIDTH, CONV_DIM), CONV_WIDTH ** -0.5),
        'ev_gate_b': jnp.concatenate([i_bias, f_bias], axis=2).reshape(N_EVEN, 4 * MLSTM_HEADS),
        'ev_w_out': nrm(ks[12], (N_EVEN, EVEN_MIX_DIM, D_MODEL), DEEPNORM_BETA * EVEN_MIX_DIM ** -0.5),
        'od_w_in': nrm(ks[13], (N_ODD, D_MODEL, ODD_IN_DIM), D_MODEL ** -0.5),
        'od_q_norm': 1.0 + nrm(ks[14], (N_ODD, Q_LORA_RANK), 0.01),
        'od_w_uq': nrm(ks[15], (N_ODD, Q_LORA_RANK, MLA_HEADS * (QK_NOPE_DIM + QK_ROPE_DIM)), Q_LORA_RANK ** -0.5),
        'od_kv_norm': 1.0 + nrm(ks[16], (N_ODD, KV_LORA_RANK), 0.01),
        'od_w_ukv': nrm(ks[17], (N_ODD, KV_LORA_RANK, MLA_HEADS * (QK_NOPE_DIM + V_HEAD_DIM)), KV_LORA_RANK ** -0.5),
        'od_w_out': nrm(ks[18], (N_ODD, MLA_DIM, D_MODEL), DEEPNORM_BETA * MLA_DIM ** -0.5),
    }


def reference(x, c, ctx, c_ctx, mod_w, mod_b, ln_g, ln_b, ev_w_in, ev_conv_w, ev_gate_b, ev_w_out,
              od_w_in, od_q_norm, od_w_uq, od_kv_norm, od_w_ukv, od_w_out):
    cos, sin = axial_rope_tables(x.shape[1])
    silu_c = jax.nn.silu(c)
    silu_cc = jax.nn.silu(c_ctx)
    h_lat, h_ctx = x, ctx
    for layer in range(DEPTH):
        need_ctx = layer < DEPTH - 1
        sh_l, sc_l, g_l = jnp.split((silu_c @ mod_w[layer] + mod_b[layer])[:, None, :], 3, axis=-1)
        sh_c, sc_c, g_c = jnp.split(silu_cc @ mod_w[layer] + mod_b[layer], 3, axis=-1)
        u_lat = h_lat * (1.0 + sc_l) + sh_l
        u_ctx = h_ctx * (1.0 + sc_c) + sh_c
        j = layer // 2
        if layer % 2 == 0:
            y_ctx, y_lat = even_mixer(u_ctx, u_lat, ev_w_in[j], ev_conv_w[j], ev_gate_b[j], ev_w_out[j], need_ctx)
        else:
            y_ctx, y_lat = mla_mixer(u_ctx, u_lat, od_w_in[j], od_q_norm[j], od_w_uq[j], od_kv_norm[j],
                                     od_w_ukv[j], od_w_out[j], cos, sin, need_ctx)
        h_lat = layer_norm(DEEPNORM_ALPHA * h_lat + g_l * y_lat, ln_g[layer], ln_b[layer])
        if need_ctx:
            h_ctx = layer_norm(DEEPNORM_ALPHA * h_ctx + g_c * y_ctx, ln_g[layer], ln_b[layer])
    return h_lat
```

```python
import functools

import jax
import jax.numpy as jnp
from jax import lax
from jax.experimental import pallas as pl
from jax.experimental.pallas import tpu as pltpu

DEPTH = 4
GRID_W = 64
CONV_DIM = 1024
MLSTM_HEADS = 4
MLSTM_HEAD_DIM = 256
MLSTM_DIM = MLSTM_HEADS * MLSTM_HEAD_DIM
MLSTM_K_SCALE = MLSTM_HEAD_DIM ** -0.5
NEG_INIT = -1e30
MLA_HEADS = 16
QK_NOPE_DIM = 128
QK_ROPE_DIM = 64
V_HEAD_DIM = 128
Q_LORA_RANK = 384
KV_LORA_RANK = 256
MLA_DIM = MLA_HEADS * V_HEAD_DIM
MLA_SCALE = (QK_NOPE_DIM + QK_ROPE_DIM) ** -0.5
ROPE_BASE = 10000.0
LN_EPS = 1e-5
RMS_EPS = 1e-6
DEEPNORM_ALPHA = (2 * DEPTH) ** 0.25

V7X_LANES = 128
V7X_MXU_DIM = 256
V7X_BF16_SUBLANES = 16
V7X_VMEM_BYTES = 64 * 1024 * 1024

ROW_TILE = V7X_MXU_DIM
INPROJ_ROWS = 1024
PART = 1024
N_PARTS = 9
GATE_PAD = V7X_LANES
Q_HEAD_PAD = 2 * V7X_LANES
MOD_ROWS_PAD = 8

F32 = jnp.float32
BF16 = jnp.bfloat16


def _vmem_limit(nbytes):
    return int(min(nbytes + 16 * 1024 * 1024, V7X_VMEM_BYTES - 4 * 1024 * 1024))


def _params(ngrid, nbytes):
    return pltpu.CompilerParams(
        dimension_semantics=("arbitrary",) * ngrid,
        vmem_limit_bytes=_vmem_limit(nbytes))


def _silu(x):
    return x * jax.nn.sigmoid(x)


def _dot(a, b):
    return jnp.dot(a, b, preferred_element_type=F32)


def _dot_nt(a, b):
    return lax.dot_general(a, b, (((1,), (1,)), ((), ())), preferred_element_type=F32)


def _mod_kernel(cc_ref, w_ref, b_ref, o_ref):
    cc = cc_ref[...]
    o_ref[0] = _dot(_silu(cc), w_ref[0]) + b_ref[0]


def _modulation(cc, mod_w, mod_b):
    rows, d = cc.shape
    depth, _, n3 = mod_w.shape
    nblk = n3 // d
    return pl.pallas_call(
        _mod_kernel,
        grid=(depth, nblk),
        in_specs=[
            pl.BlockSpec((rows, d), lambda l, n: (0, 0)),
            pl.BlockSpec((1, d, d), lambda l, n: (l, 0, n)),
            pl.BlockSpec((1, 1, d), lambda l, n: (l, 0, n)),
        ],
        out_specs=pl.BlockSpec((1, rows, d), lambda l, n: (l, 0, n)),
        out_shape=jax.ShapeDtypeStruct((depth, rows, n3), F32),
        compiler_params=_params(2, 4 * (rows * d * 4 + d * d * 4)),
        name="modulation",
    )(cc, mod_w, mod_b.reshape(depth, 1, n3))


def _prep_kernel(ctx_ref, x_ref, mod_ref, h_ref, u_ref, *, n_ctx_tiles, d):
    j = pl.program_id(1)
    val = jnp.where(j < n_ctx_tiles, ctx_ref[0], x_ref[0])
    mod = mod_ref[0]
    sh, sc = mod[:, 0:d], mod[:, d:2 * d]
    h_ref[0] = val
    u_ref[0] = (val * (1.0 + sc) + sh).astype(BF16)


def _mod_row_map(n_ctx_tiles, ctx_row):
    return lambda b, j: (jnp.where(j < n_ctx_tiles, ctx_row, b), 0, 0)


def _prep(x, ctx, mod0):
    bsz, t_len, d = x.shape
    c_len = ctx.shape[1]
    s_len = c_len + t_len
    nct, nt = c_len // ROW_TILE, s_len // ROW_TILE
    kern = functools.partial(_prep_kernel, n_ctx_tiles=nct, d=d)
    tile = (1, ROW_TILE, d)
    return pl.pallas_call(
        kern,
        grid=(bsz, nt),
        in_specs=[
            pl.BlockSpec(tile, lambda b, j: (b, jnp.minimum(j, nct - 1), 0)),
            pl.BlockSpec(tile, lambda b, j: (b, jnp.maximum(j - nct, 0), 0)),
            pl.BlockSpec((1, 1, 3 * d), _mod_row_map(nct, bsz)),
        ],
        out_specs=[pl.BlockSpec(tile, lambda b, j: (b, j, 0)),
                   pl.BlockSpec(tile, lambda b, j: (b, j, 0))],
        out_shape=[jax.ShapeDtypeStruct((bsz, s_len, d), F32),
                   jax.ShapeDtypeStruct((bsz, s_len, d), BF16)],
        compiler_params=_params(2, 2 * ROW_TILE * d * (4 + 4 + 4 + 2)),
        name="prep",
    )(ctx, x, mod0)


def _even_inproj_kernel(u_ref, w_ref, wg_ref, gb_ref, main_ref, g_ref):
    n = pl.program_id(1)
    u = u_ref[...]
    main_ref[...] = _dot(u, w_ref[n]).astype(BF16)

    @pl.when(n == 0)
    def _():
        g_ref[...] = _dot(u, wg_ref[...]) + gb_ref[...]


def _even_inproj(u2d, w_main, w_gate, gate_b):
    rows, d = u2d.shape
    tm = INPROJ_ROWS
    vm = (2 * tm * d * 2 + N_PARTS * d * PART * 2 + 2 * tm * PART * 2
          + tm * PART * 4 + 2 * tm * GATE_PAD * 4 + 2 * d * GATE_PAD * 2)
    return pl.pallas_call(
        _even_inproj_kernel,
        grid=(rows // tm, N_PARTS),
        in_specs=[
            pl.BlockSpec((tm, d), lambda i, n: (i, 0)),
            pl.BlockSpec((N_PARTS, d, PART), lambda i, n: (0, 0, 0),
                         pipeline_mode=pl.Buffered(1)),
            pl.BlockSpec((d, GATE_PAD), lambda i, n: (0, 0)),
            pl.BlockSpec((1, GATE_PAD), lambda i, n: (0, 0)),
        ],
        out_specs=[pl.BlockSpec((tm, PART), lambda i, n: (i, n)),
                   pl.BlockSpec((tm, GATE_PAD), lambda i, n: (i, 0))],
        out_shape=[jax.ShapeDtypeStruct((rows, N_PARTS * PART), BF16),
                   jax.ShapeDtypeStruct((rows, GATE_PAD), F32)],
        compiler_params=_params(2, vm),
        name="even_inproj",
    )(u2d, w_main, w_gate, gate_b)


def _log_sigmoid(x):
    return jnp.minimum(x, 0.0) - jnp.log1p(jnp.exp(-jnp.abs(x)))


def _lane_scan(x, op, ident, reverse):
    n = x.shape[1]
    lane = lax.broadcasted_iota(jnp.int32, x.shape, 1)
    k = 1
    while k < n:
        if reverse:
            shifted = jnp.where(lane < n - k, pltpu.roll(x, n - k, 1), ident)
        else:
            shifted = jnp.where(lane >= k, pltpu.roll(x, k, 1), ident)
        x = op(x, shifted)
        k *= 2
    return x


def _gate_prep_kernel(g_ref, c_ref, r_ref, *, n_chunks, chunk):
    nh = MLSTM_HEADS

    def body(c, carry):
        r0 = pl.multiple_of(c * chunk, chunk)
        gt = g_ref[0, pl.ds(r0, chunk), :].T
        i_f, f_f = gt[0:nh], gt[nh:2 * nh]
        i_b, f_b = gt[2 * nh:3 * nh], gt[3 * nh:4 * nh]
        b_f = _lane_scan(_log_sigmoid(f_f), jnp.add, 0.0, False)
        b_b = _lane_scan(_log_sigmoid(f_b), jnp.add, 0.0, True)
        a_f, a_b = i_f - b_f, i_b - b_b
        cm_f = _lane_scan(a_f, jnp.maximum, -jnp.inf, False)
        cm_b = _lane_scan(a_b, jnp.maximum, -jnp.inf, True)
        sub = lax.broadcasted_iota(jnp.int32, (8, chunk), 0)
        pad = jnp.zeros((V7X_LANES - 8, chunk), F32)
        for h in range(nh):
            rows = (b_f, a_f, cm_f, b_b, a_b, cm_b)
            x8 = jnp.zeros((8, chunk), F32)
            for idx, arr in enumerate(rows):
                x8 = jnp.where(sub == idx, arr[h:h + 1, :], x8)
            r_ref[0, h, c] = x8
            c_ref[0, h, pl.ds(r0, chunk), :] = jnp.concatenate([x8, pad], axis=0).T
        return carry

    lax.fori_loop(0, n_chunks, body, 0)


def _gate_prep(gates):
    bsz, s_len, gp = gates.shape
    chunk = ROW_TILE
    nch = s_len // chunk
    nh = MLSTM_HEADS
    kern = functools.partial(_gate_prep_kernel, n_chunks=nch, chunk=chunk)
    vm = 2 * s_len * gp * 4 * (1 + nh) + 2 * nh * nch * 8 * chunk * 4
    return pl.pallas_call(
        kern,
        grid=(bsz,),
        in_specs=[pl.BlockSpec((1, s_len, gp), lambda b: (b, 0, 0))],
        out_specs=[pl.BlockSpec((1, nh, s_len, V7X_LANES), lambda b: (b, 0, 0, 0)),
                   pl.BlockSpec((1, nh, nch, 8, chunk), lambda b: (b, 0, 0, 0, 0))],
        out_shape=[jax.ShapeDtypeStruct((bsz, nh, s_len, V7X_LANES), F32),
                   jax.ShapeDtypeStruct((bsz, nh, nch, 8, chunk), F32)],
        compiler_params=_params(1, vm),
        name="gate_prep",
    )(gates)


def _mlstm_kernel(q_ref, k_ref, v_ref, o_ref, z_ref, c_ref, r_ref, out_ref,
                  hacc_ref, cst_ref, nst_ref, *, n_chunks, n_ctx_chunks, chunk):
    t_idx = lax.broadcasted_iota(jnp.int32, (chunk, chunk), 0)
    s_idx = lax.broadcasted_iota(jnp.int32, (chunk, chunk), 1)

    for direction in range(2):
        mask = (s_idx <= t_idx) if direction == 0 else (s_idx >= t_idx)
        last = chunk - 1 if direction == 0 else 0
        col0 = 3 * direction
        cst_ref[...] = jnp.zeros_like(cst_ref)
        nst_ref[...] = jnp.zeros_like(nst_ref)

        def body(jj, m_state, direction=direction, mask=mask, last=last, col0=col0):
            if direction == 0:
                ci = jj
            else:
                ci = jnp.where(jj < n_ctx_chunks, n_ctx_chunks - 1 - jj,
                               n_chunks - 1 - (jj - n_ctx_chunks))
            rows = pl.ds(pl.multiple_of(ci * chunk, chunk), chunk)
            q = q_ref[0, rows, :]
            k = k_ref[0, rows, :] * jnp.asarray(MLSTM_K_SCALE, BF16)
            v = v_ref[0, rows, :]
            cols = c_ref[0, 0, rows, :]
            b_col = cols[:, col0:col0 + 1]
            a_col = cols[:, col0 + 1:col0 + 2]
            cm_col = cols[:, col0 + 2:col0 + 3]
            a_row = r_ref[0, 0, ci][col0 + 1:col0 + 2, :]

            m_t = jnp.maximum(m_state, cm_col)
            w = jnp.exp(jnp.where(mask, a_row - m_t, -jnp.inf))
            a_inter = jnp.exp(m_state - m_t)
            s = _dot_nt(q, k) * w
            c_prev = cst_ref[...]
            n_prev = nst_ref[...]
            num = _dot(s.astype(BF16), v) + a_inter * _dot(q, c_prev.astype(BF16))
            den = jnp.sum(s + a_inter * (q.astype(F32) * n_prev), axis=1, keepdims=True)
            h = num / jnp.maximum(jnp.abs(den), jnp.exp(-(b_col + m_t)))

            m_last = m_t[last:last + 1, :]
            decay = jnp.exp(m_state - m_last)
            kw = k.astype(F32) * jnp.exp(a_col - m_last)
            cst_ref[...] = decay * c_prev + _dot(kw.T.astype(BF16), v)
            nst_ref[...] = decay * n_prev + jnp.sum(kw, axis=0, keepdims=True)

            if direction == 0:
                hacc_ref[rows, :] = h
            else:
                og = o_ref[0, rows, :].astype(F32)
                zg = z_ref[0, rows, :].astype(F32)
                out_ref[0, rows, :] = (jax.nn.sigmoid(og) * (hacc_ref[rows, :] + h)
                                       * _silu(zg)).astype(BF16)
            return b_col[last:last + 1, :] + m_last

        lax.fori_loop(0, n_chunks, body, jnp.full((1, 1), NEG_INIT, F32))


def _mlstm(main, gate_cols, gate_rows, n_ctx_chunks):
    bsz, s_len, _ = main.shape
    nh, dh = MLSTM_HEADS, MLSTM_HEAD_DIM
    chunk = ROW_TILE
    nch = s_len // chunk
    hb = PART // dh

    def part_spec(part):
        return pl.BlockSpec((1, s_len, dh), lambda b, h: (b, 0, part * hb + h))

    kern = functools.partial(_mlstm_kernel, n_chunks=nch, n_ctx_chunks=n_ctx_chunks, chunk=chunk)
    vm = (2 * 6 * s_len * dh * 2 + 2 * s_len * V7X_LANES * 4 + 2 * nch * 8 * chunk * 4
          + s_len * dh * 4 + dh * dh * 4 + 8 * chunk * chunk * 4)
    return pl.pallas_call(
        kern,
        grid=(bsz, nh),
        in_specs=[part_spec(4), part_spec(5), part_spec(6), part_spec(7), part_spec(8),
                  pl.BlockSpec((1, 1, s_len, V7X_LANES), lambda b, h: (b, h, 0, 0)),
                  pl.BlockSpec((1, 1, nch, 8, chunk), lambda b, h: (b, h, 0, 0, 0))],
        out_specs=pl.BlockSpec((1, s_len, dh), lambda b, h: (b, 0, h)),
        out_shape=jax.ShapeDtypeStruct((bsz, s_len, nh * dh), BF16),
        scratch_shapes=[pltpu.VMEM((s_len, dh), F32),
                        pltpu.VMEM((dh, dh), F32),
                        pltpu.VMEM((1, dh), F32)],
        compiler_params=_params(2, vm),
        name="mlstm",
    )(main, main, main, main, main, gate_cols, gate_rows)


def _conv_kernel(ab_ref, ac_ref, ax_ref, az_ref, acp_ref, axp_ref, acn_ref, axn_ref,
                 w_ref, out_ref, *, n_ctx_tiles, n_tiles, tile):
    j = pl.program_id(1)
    starts = jnp.logical_or(j == 0, j == n_ctx_tiles)
    ends = jnp.logical_or(j == n_ctx_tiles - 1, j == n_tiles - 1)
    p = ac_ref[0].astype(F32) * ax_ref[0].astype(F32)
    hb = V7X_BF16_SUBLANES
    prev = (acp_ref[0].astype(F32) * axp_ref[0].astype(F32))[hb - 1:hb, :]
    nxt = (acn_ref[0].astype(F32) * axn_ref[0].astype(F32))[0:1, :]
    prev = prev * jnp.where(starts, 0.0, 1.0)
    nxt = nxt * jnp.where(ends, 0.0, 1.0)
    row = lax.broadcasted_iota(jnp.int32, p.shape, 0)
    p_dn = jnp.where(row == 0, prev, pltpu.roll(p, 1, 0))
    p_up = jnp.where(row == tile - 1, nxt, pltpu.roll(p, tile - 1, 0))
    w = w_ref[...]
    conv = w[0:1, :] * p_dn + w[1:2, :] * p + w[2:3, :] * p_up
    out_ref[0] = (ab_ref[0].astype(F32) * conv * _silu(az_ref[0].astype(F32))).astype(BF16)


def _conv_branch(main, conv_w, n_ctx_tiles):
    bsz, s_len, _ = main.shape
    tile = ROW_TILE
    nt = s_len // tile
    hb = V7X_BF16_SUBLANES
    per = tile // hb
    n_halo = s_len // hb

    def part_spec(part):
        return pl.BlockSpec((1, tile, PART), lambda b, j: (b, j, part))

    def prev_spec(part):
        return pl.BlockSpec((1, hb, PART), lambda b, j: (b, jnp.maximum(j * per - 1, 0), part))

    def next_spec(part):
        return pl.BlockSpec((1, hb, PART),
                            lambda b, j: (b, jnp.minimum((j + 1) * per, n_halo - 1), part))

    kern = functools.partial(_conv_kernel, n_ctx_tiles=n_ctx_tiles, n_tiles=nt, tile=tile)
    vm = 2 * 5 * tile * PART * 2 + 8 * tile * PART * 4
    return pl.pallas_call(
        kern,
        grid=(bsz, nt),
        in_specs=[part_spec(0), part_spec(1), part_spec(2), part_spec(3),
                  prev_spec(1), prev_spec(2), next_spec(1), next_spec(2),
                  pl.BlockSpec(conv_w.shape, lambda b, j: (0, 0))],
        out_specs=pl.BlockSpec((1, tile, PART), lambda b, j: (b, j, 0)),
        out_shape=jax.ShapeDtypeStruct((bsz, s_len, PART), BF16),
        compiler_params=_params(2, vm),
        name="conv_branch",
    )(main, main, main, main, main, main, main, main, conv_w)


def _outproj_kernel(*refs, d, has_next):
    if has_next:
        a1_ref, a2_ref, w_ref, h_ref, modc_ref, lng_ref, lnb_ref, modn_ref, hn_ref, u_ref = refs
    else:
        a1_ref, a2_ref, w_ref, h_ref, modc_ref, lng_ref, lnb_ref, hn_ref = refs
    y = _dot(a1_ref[0], w_ref[0]) + _dot(a2_ref[0], w_ref[1])
    gate = modc_ref[0][:, 2 * d:3 * d]
    r = DEEPNORM_ALPHA * h_ref[0] + gate * y
    mu = jnp.mean(r, axis=-1, keepdims=True)
    xc = r - mu
    var = jnp.mean(xc * xc, axis=-1, keepdims=True)
    hn = xc * lax.rsqrt(var + LN_EPS) * lng_ref[...] + lnb_ref[...]
    hn_ref[0] = hn
    if has_next:
        modn = modn_ref[0]
        u_ref[0] = (hn * (1.0 + modn[:, d:2 * d]) + modn[:, 0:d]).astype(BF16)


def _outproj(a1_arr, a1_blk, a2_arr, a2_blk, w2, h, mod_cur, mod_next, ln_g, ln_b, n_ctx_tiles):
    bsz, s_len, d = h.shape
    tile = ROW_TILE
    nt = s_len // tile
    has_next = mod_next is not None
    off = 0 if has_next else n_ctx_tiles
    mod_map = _mod_row_map(n_ctx_tiles - off, bsz)
    row_blk = lambda b, j: (b, j + off, 0)
    in_specs = [
        pl.BlockSpec((1, tile, PART), lambda b, j: (b, j + off, a1_blk)),
        pl.BlockSpec((1, tile, PART), lambda b, j: (b, j + off, a2_blk)),
        pl.BlockSpec(w2.shape, lambda b, j: (0, 0, 0)),
        pl.BlockSpec((1, tile, d), row_blk),
        pl.BlockSpec((1, 1, 3 * d), mod_map),
        pl.BlockSpec((1, d), lambda b, j: (0, 0)),
        pl.BlockSpec((1, d), lambda b, j: (0, 0)),
    ]
    args = [a1_arr, a2_arr, w2, h, mod_cur, ln_g, ln_b]
    out_specs = [pl.BlockSpec((1, tile, d), lambda b, j: (b, j, 0))]
    out_shape = [jax.ShapeDtypeStruct((bsz, s_len - off * tile, d), F32)]
    if has_next:
        in_specs.append(pl.BlockSpec((1, 1, 3 * d), mod_map))
        args.append(mod_next)
        out_specs.append(pl.BlockSpec((1, tile, d), lambda b, j: (b, j, 0)))
        out_shape.append(jax.ShapeDtypeStruct((bsz, s_len, d), BF16))
    kern = functools.partial(_outproj_kernel, d=d, has_next=has_next)
    vm = 2 * (2 * tile * PART * 2 + 2 * PART * d * 2 + tile * d * (4 + 4 + 2)) + 6 * tile * d * 4
    res = pl.pallas_call(
        kern,
        grid=(bsz, nt - off),
        in_specs=in_specs,
        out_specs=out_specs,
        out_shape=out_shape,
        compiler_params=_params(2, vm),
        name="outproj_norm",
    )(*args)
    return res if has_next else res[0]


def _rope_pair(x, cs):
    prod = x * cs
    return prod + pltpu.roll(prod, QK_ROPE_DIM, 1)


def _mla_inproj_kernel(u_ref, win_ref, qn_ref, kvn_ref, wuq_ref, wukv_ref, cs_ref,
                       qp_ref, kn_ref, v_ref, krt_ref, z_ref):
    ql, kvl, lanes = Q_LORA_RANK, KV_LORA_RANK, V7X_LANES
    t = _dot(u_ref[0], win_ref[...])
    q_c = t[:, 0:ql]
    kv_c = t[:, ql:ql + kvl]
    kr2 = t[:, ql + kvl:ql + kvl + lanes]
    z_ref[0] = t[:, ql + kvl + lanes:].astype(BF16)
    cs = cs_ref[...]

    lane = lax.broadcasted_iota(jnp.int32, kr2.shape, 1)
    krt_ref[0] = jnp.where(lane < QK_ROPE_DIM, _rope_pair(kr2, cs), 0.0).astype(BF16)

    qn = q_c * lax.rsqrt(jnp.mean(q_c * q_c, axis=-1, keepdims=True) + RMS_EPS) * qn_ref[...]
    q = _dot(qn.astype(BF16), wuq_ref[...])
    for h in range(MLA_HEADS):
        base = h * Q_HEAD_PAD
        qp_ref[0, :, base:base + lanes] = (q[:, base:base + lanes] * MLA_SCALE).astype(BF16)
        qp_ref[0, :, base + lanes:base + 2 * lanes] = (
            _rope_pair(q[:, base + lanes:base + 2 * lanes], cs) * MLA_SCALE).astype(BF16)

    kvn = kv_c * lax.rsqrt(jnp.mean(kv_c * kv_c, axis=-1, keepdims=True) + RMS_EPS) * kvn_ref[...]
    kv = _dot(kvn.astype(BF16), wukv_ref[...])
    half = MLA_HEADS * QK_NOPE_DIM
    kn_ref[0] = kv[:, 0:half].astype(BF16)
    v_ref[0] = kv[:, half:].astype(BF16)


def _mla_inproj(u, w_in, q_norm, kv_norm, w_uq, w_ukv, cs):
    bsz, s_len, d = u.shape
    tile = ROW_TILE
    nt = s_len // tile
    n_in = w_in.shape[1]
    nq = w_uq.shape[1]
    nkv = w_ukv.shape[1]
    half = nkv // 2
    zdim = MLA_DIM
    full = lambda a: pl.BlockSpec(a.shape, lambda b, j: (0,) * a.ndim)
    row = lambda w: pl.BlockSpec((1, tile, w), lambda b, j: (b, j, 0))
    vm = (2 * (tile * d * 2 + d * n_in * 2 + w_uq.size * 2 + w_ukv.size * 2
               + tile * (nq + nkv + V7X_LANES + zdim) * 2)
          + tile * (n_in + nq + nkv) * 4 * 2)
    return pl.pallas_call(
        _mla_inproj_kernel,
        grid=(bsz, nt),
        in_specs=[row(d), full(w_in), full(q_norm), full(kv_norm), full(w_uq), full(w_ukv),
                  pl.BlockSpec((tile, V7X_LANES), lambda b, j: (j, 0))],
        out_specs=[row(nq), row(half), row(half), row(V7X_LANES), row(zdim)],
        out_shape=[jax.ShapeDtypeStruct((bsz, s_len, nq), BF16),
                   jax.ShapeDtypeStruct((bsz, s_len, half), BF16),
                   jax.ShapeDtypeStruct((bsz, s_len, half), BF16),
                   jax.ShapeDtypeStruct((bsz, s_len, V7X_LANES), BF16),
                   jax.ShapeDtypeStruct((bsz, s_len, zdim), BF16)],
        compiler_params=_params(2, vm),
        name="mla_inproj",
    )(u, w_in, q_norm, kv_norm, w_uq, w_ukv, cs)


def _mla_attn_kernel(q_ref, kn_ref, krt_ref, v_ref, z_ref, o_ref, kp_ref,
                     *, q_off, n_ctx_tiles, c_len, s_len):
    jq = pl.program_id(2)

    @pl.when(jq == 0)
    def _():
        kp_ref[:, 0:V7X_LANES] = kn_ref[0]
        kp_ref[:, V7X_LANES:2 * V7X_LANES] = krt_ref[0]

    def attend(n_keys):
        s = _dot_nt(q_ref[0], kp_ref[0:n_keys, :])
        p = jnp.exp(s - jnp.max(s, axis=1, keepdims=True))
        o = _dot(p.astype(BF16), v_ref[0, 0:n_keys, :]) / jnp.sum(p, axis=1, keepdims=True)
        o_ref[0] = (o * _silu(z_ref[0].astype(F32))).astype(BF16)

    if q_off < n_ctx_tiles:
        @pl.when(jq + q_off < n_ctx_tiles)
        def _():
            attend(c_len)

        @pl.when(jq + q_off >= n_ctx_tiles)
        def _():
            attend(s_len)
    else:
        attend(s_len)


def _mla_attention(qp, kn, krt, v, z, n_ctx_tiles, need_ctx):
    bsz, s_len, _ = qp.shape
    tile = ROW_TILE
    nt = s_len // tile
    q_off = 0 if need_ctx else n_ctx_tiles
    kern = functools.partial(_mla_attn_kernel, q_off=q_off, n_ctx_tiles=n_ctx_tiles,
                             c_len=n_ctx_tiles * tile, s_len=s_len)
    lanes = V7X_LANES
    vm = (2 * (tile * Q_HEAD_PAD * 2 + 3 * s_len * lanes * 2 + 2 * tile * lanes * 2)
          + s_len * Q_HEAD_PAD * 2 + 3 * tile * s_len * 4)
    return pl.pallas_call(
        kern,
        grid=(bsz, MLA_HEADS, nt - q_off),
        in_specs=[
            pl.BlockSpec((1, tile, Q_HEAD_PAD), lambda b, h, j: (b, j + q_off, h)),
            pl.BlockSpec((1, s_len, lanes), lambda b, h, j: (b, 0, h)),
            pl.BlockSpec((1, s_len, lanes), lambda b, h, j: (b, 0, 0)),
            pl.BlockSpec((1, s_len, lanes), lambda b, h, j: (b, 0, h)),
            pl.BlockSpec((1, tile, lanes), lambda b, h, j: (b, j + q_off, h)),
        ],
        out_specs=pl.BlockSpec((1, tile, lanes), lambda b, h, j: (b, j + q_off, h)),
        out_shape=jax.ShapeDtypeStruct((bsz, s_len, MLA_DIM), BF16),
        scratch_shapes=[pltpu.VMEM((s_len, Q_HEAD_PAD), BF16)],
        compiler_params=_params(3, vm),
        name="mla_attention",
    )(qp, kn, krt, v, z)


def _even_weights(w_in, gate_b, w_out):
    d = w_in.shape[0]
    n_main = N_PARTS * PART
    ng = w_in.shape[1] - n_main
    w_main = w_in[:, :n_main].reshape(d, N_PARTS, PART).transpose(1, 0, 2).astype(BF16)
    w_gate = jnp.pad(w_in[:, n_main:], ((0, 0), (0, GATE_PAD - ng))).astype(BF16)
    gb = jnp.pad(gate_b, (0, GATE_PAD - ng)).reshape(1, GATE_PAD).astype(F32)
    w_out2 = w_out.reshape(2, PART, w_out.shape[1]).astype(BF16)
    return w_main, w_gate, gb, w_out2


def _swap_halves(w):
    half = w.shape[-1] // 2
    return jnp.concatenate([w[..., half:], w[..., :half]], axis=-1)


def _odd_weights(w_in, w_uq, w_ukv, w_out):
    ql, kvl, rope = Q_LORA_RANK, KV_LORA_RANK, QK_ROPE_DIM
    w_kr = w_in[:, ql + kvl:ql + kvl + rope]
    w_in_x = jnp.concatenate([w_in[:, :ql + kvl], w_kr, _swap_halves(w_kr),
                              w_in[:, ql + kvl + rope:]], axis=1).astype(BF16)
    wq = w_uq.reshape(ql, MLA_HEADS, QK_NOPE_DIM + rope)
    wq_r = wq[:, :, QK_NOPE_DIM:]
    w_uq_x = jnp.concatenate([wq[:, :, :QK_NOPE_DIM], wq_r, _swap_halves(wq_r)], axis=2)
    w_uq_x = w_uq_x.reshape(ql, MLA_HEADS * Q_HEAD_PAD).astype(BF16)
    wkv = w_ukv.reshape(kvl, MLA_HEADS, QK_NOPE_DIM + V_HEAD_DIM)
    w_ukv_x = jnp.concatenate([wkv[:, :, :QK_NOPE_DIM].reshape(kvl, -1),
                               wkv[:, :, QK_NOPE_DIM:].reshape(kvl, -1)], axis=1).astype(BF16)
    w_out2 = w_out.reshape(2, PART, w_out.shape[1]).astype(BF16)
    return w_in_x, w_uq_x, w_ukv_x, w_out2


def _rope_table(t_len, c_len):
    rows = t_len // GRID_W
    row = jnp.repeat(jnp.arange(rows, dtype=F32), GRID_W)
    col = jnp.tile(jnp.arange(GRID_W, dtype=F32), rows)
    n_freq = QK_ROPE_DIM // 4
    inv_freq = ROPE_BASE ** (-jnp.arange(n_freq, dtype=F32) / n_freq)
    ang = jnp.concatenate([row[:, None] * inv_freq, col[:, None] * inv_freq], axis=-1)
    cos, sin = jnp.cos(ang), jnp.sin(ang)
    lat = jnp.concatenate([cos, cos, -sin, sin], axis=-1)
    half = QK_ROPE_DIM
    ctx = jnp.concatenate([jnp.ones((c_len, half), F32), jnp.zeros((c_len, half), F32)], axis=-1)
    return jnp.concatenate([ctx, lat], axis=0)


def kernel(x, c, ctx, c_ctx, mod_w, mod_b, ln_g, ln_b, ev_w_in, ev_conv_w, ev_gate_b, ev_w_out,
           od_w_in, od_q_norm, od_w_uq, od_kv_norm, od_w_ukv, od_w_out):
    bsz, t_len, d = x.shape
    c_len = ctx.shape[1]
    s_len = c_len + t_len
    assert d == PART and c_len % ROW_TILE == 0 and t_len % ROW_TILE == 0
    assert (bsz * s_len) % INPROJ_ROWS == 0 and t_len % GRID_W == 0
    nct = c_len // ROW_TILE

    n_rows = -(-(bsz + 1) // MOD_ROWS_PAD) * MOD_ROWS_PAD
    cc = jnp.concatenate([c, c_ctx[None, :], jnp.zeros((n_rows - bsz - 1, d), F32)], axis=0)
    mods = _modulation(cc, mod_w, mod_b).reshape(DEPTH, n_rows, 1, 3 * d)

    cs = _rope_table(t_len, c_len)
    h, u = _prep(x, ctx, mods[0])

    for layer in range(DEPTH):
        j = layer // 2
        last = layer == DEPTH - 1
        mod_next = None if last else mods[layer + 1]
        lng, lnb = ln_g[layer].reshape(1, d), ln_b[layer].reshape(1, d)
        if layer % 2 == 0:
            w_main, w_gate, gb, w_out2 = _even_weights(ev_w_in[j], ev_gate_b[j], ev_w_out[j])
            main, gates = _even_inproj(u.reshape(bsz * s_len, d), w_main, w_gate, gb)
            main = main.reshape(bsz, s_len, N_PARTS * PART)
            gate_cols, gate_rows = _gate_prep(gates.reshape(bsz, s_len, GATE_PAD))
            mix_b = _mlstm(main, gate_cols, gate_rows, nct)
            mix_a = _conv_branch(main, ev_conv_w[j], nct)
            res = _outproj(mix_a, 0, mix_b, 0, w_out2, h, mods[layer], mod_next, lng, lnb, nct)
        else:
            w_in_x, w_uq_x, w_ukv_x, w_out2 = _odd_weights(od_w_in[j], od_w_uq[j], od_w_ukv[j],
                                                           od_w_out[j])
            qp, kn, v, krt, z = _mla_inproj(u, w_in_x, od_q_norm[j].reshape(1, -1),
                                            od_kv_norm[j].reshape(1, -1), w_uq_x, w_ukv_x, cs)
            og = _mla_attention(qp, kn, krt, v, z, nct, need_ctx=not last)
            res = _outproj(og, 0, og, 1, w_out2, h, mods[layer], mod_next, lng, lnb, nct)
        if last:
            return res
        h, u = res
```

```python
import functools

import jax
import jax.numpy as jnp
from jax import lax
from jax.experimental import pallas as pl
from jax.experimental.pallas import tpu as pltpu

DEPTH = 4
GRID_W = 64
CONV_DIM = 1024
MLSTM_HEADS = 4
MLSTM_HEAD_DIM = 256
MLSTM_DIM = MLSTM_HEADS * MLSTM_HEAD_DIM
MLSTM_K_SCALE = MLSTM_HEAD_DIM ** -0.5
NEG_INIT = -1e30
MLA_HEADS = 16
QK_NOPE_DIM = 128
QK_ROPE_DIM = 64
V_HEAD_DIM = 128
Q_LORA_RANK = 384
KV_LORA_RANK = 256
MLA_DIM = MLA_HEADS * V_HEAD_DIM
MLA_SCALE = (QK_NOPE_DIM + QK_ROPE_DIM) ** -0.5
ROPE_BASE = 10000.0
LN_EPS = 1e-5
RMS_EPS = 1e-6
DEEPNORM_ALPHA = (2 * DEPTH) ** 0.25

V7X_LANES = 128
V7X_MXU_DIM = 256
V7X_BF16_SUBLANES = 16
V7X_VMEM_BYTES = 64 * 1024 * 1024

ROW_TILE = V7X_MXU_DIM
INPROJ_ROWS = 1024
PART = 1024
N_PARTS = 9
GATE_PAD = V7X_LANES
Q_HEAD_PAD = 2 * V7X_LANES
MOD_ROWS_PAD = 8
MLSTM_HEADS_PER_STEP = 1
ATTN_HEADS_PER_STEP = 4
LOG2_E = 1.4426950408889634

F32 = jnp.float32
BF16 = jnp.bfloat16


def _vmem_limit(nbytes):
    return int(min(nbytes + 16 * 1024 * 1024, V7X_VMEM_BYTES - 4 * 1024 * 1024))


def _params(ngrid, nbytes):
    return pltpu.CompilerParams(
        dimension_semantics=("arbitrary",) * ngrid,
        vmem_limit_bytes=_vmem_limit(nbytes))


def _sigmoid(x):
    return 0.5 * jnp.tanh(0.5 * x) + 0.5


def _silu(x):
    return x * _sigmoid(x)


def _dot(a, b):
    return jnp.dot(a, b, preferred_element_type=F32)


def _dot_nt(a, b):
    return lax.dot_general(a, b, (((1,), (1,)), ((), ())), preferred_element_type=F32)


def _mod_kernel(cc_ref, w_ref, b_ref, o_ref):
    cc = cc_ref[...]
    o_ref[0] = _dot(_silu(cc), w_ref[0]) + b_ref[0]


def _modulation(cc, mod_w, mod_b):
    rows, d = cc.shape
    depth, _, n3 = mod_w.shape
    nblk = n3 // d
    return pl.pallas_call(
        _mod_kernel,
        grid=(depth, nblk),
        in_specs=[
            pl.BlockSpec((rows, d), lambda l, n: (0, 0)),
            pl.BlockSpec((1, d, d), lambda l, n: (l, 0, n)),
            pl.BlockSpec((1, 1, d), lambda l, n: (l, 0, n)),
        ],
        out_specs=pl.BlockSpec((1, rows, d), lambda l, n: (l, 0, n)),
        out_shape=jax.ShapeDtypeStruct((depth, rows, n3), F32),
        compiler_params=_params(2, 4 * (rows * d * 4 + d * d * 4)),
        name="modulation",
    )(cc, mod_w, mod_b.reshape(depth, 1, n3))


def _prep_kernel(ctx_ref, x_ref, mod_ref, h_ref, u_ref, *, n_ctx_tiles, d):
    j = pl.program_id(1)
    val = jnp.where(j < n_ctx_tiles, ctx_ref[0], x_ref[0])
    mod = mod_ref[0]
    sh, sc = mod[:, 0:d], mod[:, d:2 * d]
    h_ref[0] = val
    u_ref[0] = (val * (1.0 + sc) + sh).astype(BF16)


def _mod_row_map(n_ctx_tiles, ctx_row):
    return lambda b, j: (jnp.where(j < n_ctx_tiles, ctx_row, b), 0, 0)


def _prep(x, ctx, mod0):
    bsz, t_len, d = x.shape
    c_len = ctx.shape[1]
    s_len = c_len + t_len
    nct, nt = c_len // ROW_TILE, s_len // ROW_TILE
    kern = functools.partial(_prep_kernel, n_ctx_tiles=nct, d=d)
    tile = (1, ROW_TILE, d)
    return pl.pallas_call(
        kern,
        grid=(bsz, nt),
        in_specs=[
            pl.BlockSpec(tile, lambda b, j: (b, jnp.minimum(j, nct - 1), 0)),
            pl.BlockSpec(tile, lambda b, j: (b, jnp.maximum(j - nct, 0), 0)),
            pl.BlockSpec((1, 1, 3 * d), _mod_row_map(nct, bsz)),
        ],
        out_specs=[pl.BlockSpec(tile, lambda b, j: (b, j, 0)),
                   pl.BlockSpec(tile, lambda b, j: (b, j, 0))],
        out_shape=[jax.ShapeDtypeStruct((bsz, s_len, d), F32),
                   jax.ShapeDtypeStruct((bsz, s_len, d), BF16)],
        compiler_params=_params(2, 2 * ROW_TILE * d * (4 + 4 + 4 + 2)),
        name="prep",
    )(ctx, x, mod0)


def _even_inproj_kernel(u_ref, w_ref, wg_ref, gb_ref, main_ref, g_ref):
    n = pl.program_id(1)
    u = u_ref[...]
    main_ref[...] = _dot(u, w_ref[n]).astype(BF16)

    @pl.when(n == 0)
    def _():
        g_ref[...] = _dot(u, wg_ref[...]) + gb_ref[...]


def _even_inproj(u2d, w_main, w_gate, gate_b):
    rows, d = u2d.shape
    tm = INPROJ_ROWS
    vm = (2 * tm * d * 2 + N_PARTS * d * PART * 2 + 2 * tm * PART * 2
          + tm * PART * 4 + 2 * tm * GATE_PAD * 4 + 2 * d * GATE_PAD * 2)
    return pl.pallas_call(
        _even_inproj_kernel,
        grid=(rows // tm, N_PARTS),
        in_specs=[
            pl.BlockSpec((tm, d), lambda i, n: (i, 0)),
            pl.BlockSpec((N_PARTS, d, PART), lambda i, n: (0, 0, 0),
                         pipeline_mode=pl.Buffered(1)),
            pl.BlockSpec((d, GATE_PAD), lambda i, n: (0, 0)),
            pl.BlockSpec((1, GATE_PAD), lambda i, n: (0, 0)),
        ],
        out_specs=[pl.BlockSpec((tm, PART), lambda i, n: (i, n)),
                   pl.BlockSpec((tm, GATE_PAD), lambda i, n: (i, 0))],
        out_shape=[jax.ShapeDtypeStruct((rows, N_PARTS * PART), BF16),
                   jax.ShapeDtypeStruct((rows, GATE_PAD), F32)],
        compiler_params=_params(2, vm),
        name="even_inproj",
    )(u2d, w_main, w_gate, gate_b)


def _log_sigmoid(x):
    return jnp.minimum(x, 0.0) - jnp.log1p(jnp.exp(-jnp.abs(x)))


def _lane_scan(x, op, ident, reverse):
    n = x.shape[1]
    lane = lax.broadcasted_iota(jnp.int32, x.shape, 1)
    k = 1
    while k < n:
        if reverse:
            shifted = jnp.where(lane < n - k, pltpu.roll(x, n - k, 1), ident)
        else:
            shifted = jnp.where(lane >= k, pltpu.roll(x, k, 1), ident)
        x = op(x, shifted)
        k *= 2
    return x


def _gate_prep_kernel(g_ref, c_ref, r_ref, *, n_chunks, chunk):
    nh = MLSTM_HEADS

    def body(c, carry):
        r0 = pl.multiple_of(c * chunk, chunk)
        gt = g_ref[0, pl.ds(r0, chunk), :].T
        i_f, f_f = gt[0:nh], gt[nh:2 * nh]
        i_b, f_b = gt[2 * nh:3 * nh], gt[3 * nh:4 * nh]
        b_f = _lane_scan(_log_sigmoid(f_f), jnp.add, 0.0, False)
        b_b = _lane_scan(_log_sigmoid(f_b), jnp.add, 0.0, True)
        a_f, a_b = i_f - b_f, i_b - b_b
        cm_f = _lane_scan(a_f, jnp.maximum, -jnp.inf, False)
        cm_b = _lane_scan(a_b, jnp.maximum, -jnp.inf, True)
        sub = lax.broadcasted_iota(jnp.int32, (8, chunk), 0)
        pad = jnp.zeros((V7X_LANES - 8, chunk), F32)
        for h in range(nh):
            rows = (b_f, a_f, cm_f, b_b, a_b, cm_b)
            x8 = jnp.zeros((8, chunk), F32)
            for idx, arr in enumerate(rows):
                x8 = jnp.where(sub == idx, arr[h:h + 1, :], x8)
            r_ref[0, h, c] = x8
            c_ref[0, h, pl.ds(r0, chunk), :] = jnp.concatenate([x8, pad], axis=0).T
        return carry

    lax.fori_loop(0, n_chunks, body, 0)


def _gate_prep(gates):
    bsz, s_len, gp = gates.shape
    chunk = ROW_TILE
    nch = s_len // chunk
    nh = MLSTM_HEADS
    kern = functools.partial(_gate_prep_kernel, n_chunks=nch, chunk=chunk)
    vm = 2 * s_len * gp * 4 * (1 + nh) + 2 * nh * nch * 8 * chunk * 4
    return pl.pallas_call(
        kern,
        grid=(bsz,),
        in_specs=[pl.BlockSpec((1, s_len, gp), lambda b: (b, 0, 0))],
        out_specs=[pl.BlockSpec((1, nh, s_len, V7X_LANES), lambda b: (b, 0, 0, 0)),
                   pl.BlockSpec((1, nh, nch, 8, chunk), lambda b: (b, 0, 0, 0, 0))],
        out_shape=[jax.ShapeDtypeStruct((bsz, nh, s_len, V7X_LANES), F32),
                   jax.ShapeDtypeStruct((bsz, nh, nch, 8, chunk), F32)],
        compiler_params=_params(1, vm),
        name="gate_prep",
    )(gates)


def _mlstm_chunk(q, k, v, cols, a_row, m_state, c_prev, n_prev, direction, mask):
    chunk = q.shape[0]
    last = chunk - 1 if direction == 0 else 0
    col0 = 3 * direction
    b_col = cols[:, col0:col0 + 1]
    a_col = cols[:, col0 + 1:col0 + 2]
    cm_col = cols[:, col0 + 2:col0 + 3]

    m_t = jnp.maximum(m_state, cm_col)
    w = jnp.exp(jnp.where(mask, a_row - m_t, -jnp.inf))
    a_inter = jnp.exp(m_state - m_t)
    s = _dot_nt(q, k) * w
    num = _dot(s.astype(BF16), v) + a_inter * _dot(q, c_prev.astype(BF16))
    den = jnp.sum(s + a_inter * (q.astype(F32) * n_prev), axis=1, keepdims=True)
    h = num / jnp.maximum(jnp.abs(den), jnp.exp(-(b_col + m_t)))

    m_last = m_t[last:last + 1, :]
    decay = jnp.exp(m_state - m_last)
    kw = k.astype(F32) * jnp.exp(a_col - m_last)
    c_new = decay * c_prev + _dot(kw.T.astype(BF16), v)
    n_new = decay * n_prev + jnp.sum(kw, axis=0, keepdims=True)
    return h, c_new, n_new, b_col[last:last + 1, :] + m_last


def _mlstm_kernel(q_ref, k_ref, v_ref, o_ref, z_ref, c_ref, r_ref, out_ref,
                  hf_ref, hb_ref, cst_ref, nst_ref, *, n_chunks, n_ctx_chunks, chunk, heads):
    dh = MLSTM_HEAD_DIM
    t_idx = lax.broadcasted_iota(jnp.int32, (chunk, chunk), 0)
    s_idx = lax.broadcasted_iota(jnp.int32, (chunk, chunk), 1)
    masks = (s_idx <= t_idx, s_idx >= t_idx)
    h_refs = (hf_ref, hb_ref)
    cst_ref[...] = jnp.zeros_like(cst_ref)
    nst_ref[...] = jnp.zeros_like(nst_ref)

    def body(jj, m_states):
        ci_bwd = jnp.where(jj < n_ctx_chunks, n_ctx_chunks - 1 - jj,
                           n_chunks - 1 - (jj - n_ctx_chunks))
        new_states = []
        for g in range(heads):
            lanes = slice(g * dh, (g + 1) * dh)
            for direction in range(2):
                ci = jj if direction == 0 else ci_bwd
                rows = pl.ds(pl.multiple_of(ci * chunk, chunk), chunk)
                idx = 2 * g + direction
                q = q_ref[0, rows, lanes]
                k = k_ref[0, rows, lanes] * jnp.asarray(MLSTM_K_SCALE, BF16)
                v = v_ref[0, rows, lanes]
                a_row = r_ref[0, g, ci][3 * direction + 1:3 * direction + 2, :]
                h, c_new, n_new, m_new = _mlstm_chunk(
                    q, k, v, c_ref[0, g, rows, :], a_row, m_states[idx],
                    cst_ref[idx], nst_ref[idx], direction, masks[direction])
                cst_ref[idx] = c_new
                nst_ref[idx] = n_new
                h_refs[direction][rows, lanes] = h
                new_states.append(m_new)
        return tuple(new_states)

    init = tuple(jnp.full((1, 1), NEG_INIT, F32) for _ in range(2 * heads))
    lax.fori_loop(0, n_chunks, body, init)

    def gate_body(ci, carry):
        rows = pl.ds(pl.multiple_of(ci * chunk, chunk), chunk)
        og = o_ref[0, rows, :].astype(F32)
        zg = z_ref[0, rows, :].astype(F32)
        out_ref[0, rows, :] = (_sigmoid(og) * (hf_ref[rows, :] + hb_ref[rows, :])
                               * _silu(zg)).astype(BF16)
        return carry

    lax.fori_loop(0, n_chunks, gate_body, 0)


def _mlstm(main, gate_cols, gate_rows, n_ctx_chunks):
    bsz, s_len, _ = main.shape
    nh, dh = MLSTM_HEADS, MLSTM_HEAD_DIM
    heads = MLSTM_HEADS_PER_STEP
    chunk = ROW_TILE
    assert chunk == dh
    nch = s_len // chunk
    width = heads * dh
    blocks_per_part = PART // width

    def part_spec(part):
        return pl.BlockSpec((1, s_len, width), lambda b, h: (b, 0, part * blocks_per_part + h))

    kern = functools.partial(_mlstm_kernel, n_chunks=nch, n_ctx_chunks=n_ctx_chunks,
                             chunk=chunk, heads=heads)
    vm = (2 * 6 * s_len * width * 2 + 2 * heads * s_len * V7X_LANES * 4
          + 2 * heads * nch * 8 * chunk * 4 + 2 * s_len * width * 4
          + 2 * heads * (dh + 1) * dh * 4 + 16 * chunk * chunk * 4)
    return pl.pallas_call(
        kern,
        grid=(bsz, nh // heads),
        in_specs=[part_spec(4), part_spec(5), part_spec(6), part_spec(7), part_spec(8),
                  pl.BlockSpec((1, heads, s_len, V7X_LANES), lambda b, h: (b, h, 0, 0)),
                  pl.BlockSpec((1, heads, nch, 8, chunk), lambda b, h: (b, h, 0, 0, 0))],
        out_specs=pl.BlockSpec((1, s_len, width), lambda b, h: (b, 0, h)),
        out_shape=jax.ShapeDtypeStruct((bsz, s_len, nh * dh), BF16),
        scratch_shapes=[pltpu.VMEM((s_len, width), F32),
                        pltpu.VMEM((s_len, width), F32),
                        pltpu.VMEM((2 * heads, dh, dh), F32),
                        pltpu.VMEM((2 * heads, 1, dh), F32)],
        compiler_params=_params(2, vm),
        name="mlstm",
    )(main, main, main, main, main, gate_cols, gate_rows)


def _conv_kernel(ab_ref, ac_ref, ax_ref, az_ref, acp_ref, axp_ref, acn_ref, axn_ref,
                 w_ref, out_ref, *, n_ctx_tiles, n_tiles, tile):
    j = pl.program_id(1)
    starts = jnp.logical_or(j == 0, j == n_ctx_tiles)
    ends = jnp.logical_or(j == n_ctx_tiles - 1, j == n_tiles - 1)
    p = ac_ref[0].astype(F32) * ax_ref[0].astype(F32)
    hb = V7X_BF16_SUBLANES
    prev = (acp_ref[0].astype(F32) * axp_ref[0].astype(F32))[hb - 1:hb, :]
    nxt = (acn_ref[0].astype(F32) * axn_ref[0].astype(F32))[0:1, :]
    prev = prev * jnp.where(starts, 0.0, 1.0)
    nxt = nxt * jnp.where(ends, 0.0, 1.0)
    row = lax.broadcasted_iota(jnp.int32, p.shape, 0)
    p_dn = jnp.where(row == 0, prev, pltpu.roll(p, 1, 0))
    p_up = jnp.where(row == tile - 1, nxt, pltpu.roll(p, tile - 1, 0))
    w = w_ref[...]
    conv = w[0:1, :] * p_dn + w[1:2, :] * p + w[2:3, :] * p_up
    out_ref[0] = (ab_ref[0].astype(F32) * conv * _silu(az_ref[0].astype(F32))).astype(BF16)


def _conv_branch(main, conv_w, n_ctx_tiles):
    bsz, s_len, _ = main.shape
    tile = ROW_TILE
    nt = s_len // tile
    hb = V7X_BF16_SUBLANES
    per = tile // hb
    n_halo = s_len // hb

    def part_spec(part):
        return pl.BlockSpec((1, tile, PART), lambda b, j: (b, j, part))

    def prev_spec(part):
        return pl.BlockSpec((1, hb, PART), lambda b, j: (b, jnp.maximum(j * per - 1, 0), part))

    def next_spec(part):
        return pl.BlockSpec((1, hb, PART),
                            lambda b, j: (b, jnp.minimum((j + 1) * per, n_halo - 1), part))

    kern = functools.partial(_conv_kernel, n_ctx_tiles=n_ctx_tiles, n_tiles=nt, tile=tile)
    vm = 2 * 5 * tile * PART * 2 + 8 * tile * PART * 4
    return pl.pallas_call(
        kern,
        grid=(bsz, nt),
        in_specs=[part_spec(0), part_spec(1), part_spec(2), part_spec(3),
                  prev_spec(1), prev_spec(2), next_spec(1), next_spec(2),
                  pl.BlockSpec(conv_w.shape, lambda b, j: (0, 0))],
        out_specs=pl.BlockSpec((1, tile, PART), lambda b, j: (b, j, 0)),
        out_shape=jax.ShapeDtypeStruct((bsz, s_len, PART), BF16),
        compiler_params=_params(2, vm),
        name="conv_branch",
    )(main, main, main, main, main, main, main, main, conv_w)


def _outproj_kernel(*refs, d, has_next):
    if has_next:
        a1_ref, a2_ref, w_ref, h_ref, modc_ref, lng_ref, lnb_ref, modn_ref, hn_ref, u_ref = refs
    else:
        a1_ref, a2_ref, w_ref, h_ref, modc_ref, lng_ref, lnb_ref, hn_ref = refs
    y = _dot(a1_ref[0], w_ref[0]) + _dot(a2_ref[0], w_ref[1])
    gate = modc_ref[0][:, 2 * d:3 * d]
    r = DEEPNORM_ALPHA * h_ref[0] + gate * y
    mu = jnp.mean(r, axis=-1, keepdims=True)
    xc = r - mu
    var = jnp.mean(xc * xc, axis=-1, keepdims=True)
    hn = xc * lax.rsqrt(var + LN_EPS) * lng_ref[...] + lnb_ref[...]
    hn_ref[0] = hn
    if has_next:
        modn = modn_ref[0]
        u_ref[0] = (hn * (1.0 + modn[:, d:2 * d]) + modn[:, 0:d]).astype(BF16)


def _outproj(a1_arr, a1_blk, a2_arr, a2_blk, w2, h, mod_cur, mod_next, ln_g, ln_b, n_ctx_tiles):
    bsz, s_len, d = h.shape
    tile = ROW_TILE
    nt = s_len // tile
    has_next = mod_next is not None
    off = 0 if has_next else n_ctx_tiles
    mod_map = _mod_row_map(n_ctx_tiles - off, bsz)
    row_blk = lambda b, j: (b, j + off, 0)
    in_specs = [
        pl.BlockSpec((1, tile, PART), lambda b, j: (b, j + off, a1_blk)),
        pl.BlockSpec((1, tile, PART), lambda b, j: (b, j + off, a2_blk)),
        pl.BlockSpec(w2.shape, lambda b, j: (0, 0, 0)),
        pl.BlockSpec((1, tile, d), row_blk),
        pl.BlockSpec((1, 1, 3 * d), mod_map),
        pl.BlockSpec((1, d), lambda b, j: (0, 0)),
        pl.BlockSpec((1, d), lambda b, j: (0, 0)),
    ]
    args = [a1_arr, a2_arr, w2, h, mod_cur, ln_g, ln_b]
    out_specs = [pl.BlockSpec((1, tile, d), lambda b, j: (b, j, 0))]
    out_shape = [jax.ShapeDtypeStruct((bsz, s_len - off * tile, d), F32)]
    if has_next:
        in_specs.append(pl.BlockSpec((1, 1, 3 * d), mod_map))
        args.append(mod_next)
        out_specs.append(pl.BlockSpec((1, tile, d), lambda b, j: (b, j, 0)))
        out_shape.append(jax.ShapeDtypeStruct((bsz, s_len, d), BF16))
    kern = functools.partial(_outproj_kernel, d=d, has_next=has_next)
    vm = 2 * (2 * tile * PART * 2 + 2 * PART * d * 2 + tile * d * (4 + 4 + 2)) + 6 * tile * d * 4
    res = pl.pallas_call(
        kern,
        grid=(bsz, nt - off),
        in_specs=in_specs,
        out_specs=out_specs,
        out_shape=out_shape,
        compiler_params=_params(2, vm),
        name="outproj_norm",
    )(*args)
    return res if has_next else res[0]


def _rope_pair(x, cs):
    prod = x * cs
    return prod + pltpu.roll(prod, QK_ROPE_DIM, 1)


def _mla_inproj_kernel(u_ref, win_ref, qn_ref, kvn_ref, wuq_ref, wukv_ref, cs_ref,
                       qp_ref, kn_ref, v_ref, krt_ref, z_ref):
    ql, kvl, lanes = Q_LORA_RANK, KV_LORA_RANK, V7X_LANES
    t = _dot(u_ref[0], win_ref[...])
    q_c = t[:, 0:ql]
    kv_c = t[:, ql:ql + kvl]
    kr2 = t[:, ql + kvl:ql + kvl + lanes]
    z_ref[0] = t[:, ql + kvl + lanes:].astype(BF16)
    cs = cs_ref[...]

    lane = lax.broadcasted_iota(jnp.int32, kr2.shape, 1)
    krt_ref[0] = jnp.where(lane < QK_ROPE_DIM, _rope_pair(kr2, cs), 0.0).astype(BF16)

    qn = q_c * lax.rsqrt(jnp.mean(q_c * q_c, axis=-1, keepdims=True) + RMS_EPS) * qn_ref[...]
    q = _dot(qn.astype(BF16), wuq_ref[...])
    q_scale = MLA_SCALE * LOG2_E
    for h in range(MLA_HEADS):
        base = h * Q_HEAD_PAD
        qp_ref[0, :, base:base + lanes] = (q[:, base:base + lanes] * q_scale).astype(BF16)
        qp_ref[0, :, base + lanes:base + 2 * lanes] = (
            _rope_pair(q[:, base + lanes:base + 2 * lanes], cs) * q_scale).astype(BF16)

    kvn = kv_c * lax.rsqrt(jnp.mean(kv_c * kv_c, axis=-1, keepdims=True) + RMS_EPS) * kvn_ref[...]
    kv = _dot(kvn.astype(BF16), wukv_ref[...])
    half = MLA_HEADS * QK_NOPE_DIM
    kn_ref[0] = kv[:, 0:half].astype(BF16)
    v_ref[0] = kv[:, half:].astype(BF16)


def _mla_inproj(u, w_in, q_norm, kv_norm, w_uq, w_ukv, cs):
    bsz, s_len, d = u.shape
    tile = ROW_TILE
    nt = s_len // tile
    n_in = w_in.shape[1]
    nq = w_uq.shape[1]
    nkv = w_ukv.shape[1]
    half = nkv // 2
    zdim = MLA_DIM
    full = lambda a: pl.BlockSpec(a.shape, lambda b, j: (0,) * a.ndim)
    row = lambda w: pl.BlockSpec((1, tile, w), lambda b, j: (b, j, 0))
    vm = (2 * (tile * d * 2 + d * n_in * 2 + w_uq.size * 2 + w_ukv.size * 2
               + tile * (nq + nkv + V7X_LANES + zdim) * 2)
          + tile * (n_in + nq + nkv) * 4 * 2)
    return pl.pallas_call(
        _mla_inproj_kernel,
        grid=(bsz, nt),
        in_specs=[row(d), full(w_in), full(q_norm), full(kv_norm), full(w_uq), full(w_ukv),
                  pl.BlockSpec((tile, V7X_LANES), lambda b, j: (j, 0))],
        out_specs=[row(nq), row(half), row(half), row(V7X_LANES), row(zdim)],
        out_shape=[jax.ShapeDtypeStruct((bsz, s_len, nq), BF16),
                   jax.ShapeDtypeStruct((bsz, s_len, half), BF16),
                   jax.ShapeDtypeStruct((bsz, s_len, half), BF16),
                   jax.ShapeDtypeStruct((bsz, s_len, V7X_LANES), BF16),
                   jax.ShapeDtypeStruct((bsz, s_len, zdim), BF16)],
        compiler_params=_params(2, vm),
        name="mla_inproj",
    )(u, w_in, q_norm, kv_norm, w_uq, w_ukv, cs)


def _mla_attn_kernel(q_ref, kn_ref, krt_ref, v_ref, z_ref, o_ref, kp_ref,
                     *, q_off, n_ctx_tiles, c_len, s_len, heads):
    jq = pl.program_id(2)
    lanes = V7X_LANES

    @pl.when(jq == 0)
    def _():
        for g in range(heads):
            kp_ref[g, :, 0:lanes] = kn_ref[0, :, g * lanes:(g + 1) * lanes]
            kp_ref[g, :, lanes:2 * lanes] = krt_ref[0]

    def scores(g, n_keys):
        return _dot_nt(q_ref[0, :, g * Q_HEAD_PAD:(g + 1) * Q_HEAD_PAD], kp_ref[g, 0:n_keys, :])

    def softmax(s):
        p = jnp.exp2(s - jnp.max(s, axis=1, keepdims=True))
        return p.astype(BF16), jnp.sum(p, axis=1, keepdims=True)

    def values(g, p, denom, n_keys):
        o = _dot(p, v_ref[0, 0:n_keys, g * lanes:(g + 1) * lanes]) / denom
        zg = z_ref[0, :, g * lanes:(g + 1) * lanes].astype(F32)
        o_ref[0, :, g * lanes:(g + 1) * lanes] = (o * _silu(zg)).astype(BF16)

    def attend(n_keys):
        s = {0: scores(0, n_keys)}
        pl_ = {}
        for g in range(heads + 2):
            if g + 1 < heads:
                s[g + 1] = scores(g + 1, n_keys)
            if g < heads:
                pl_[g] = softmax(s.pop(g))
            if 1 <= g <= heads:
                values(g - 1, *pl_.pop(g - 1), n_keys)

    if q_off < n_ctx_tiles:
        @pl.when(jq + q_off < n_ctx_tiles)
        def _():
            attend(c_len)

        @pl.when(jq + q_off >= n_ctx_tiles)
        def _():
            attend(s_len)
    else:
        attend(s_len)


def _mla_attention(qp, kn, krt, v, z, n_ctx_tiles, need_ctx):
    bsz, s_len, _ = qp.shape
    tile = ROW_TILE
    nt = s_len // tile
    q_off = 0 if need_ctx else n_ctx_tiles
    heads = ATTN_HEADS_PER_STEP
    kern = functools.partial(_mla_attn_kernel, q_off=q_off, n_ctx_tiles=n_ctx_tiles,
                             c_len=n_ctx_tiles * tile, s_len=s_len, heads=heads)
    lanes = V7X_LANES
    vm = (2 * (heads * tile * Q_HEAD_PAD * 2 + (2 * heads + 1) * s_len * lanes * 2
               + 2 * heads * tile * lanes * 2)
          + heads * s_len * Q_HEAD_PAD * 2 + heads * 2 * tile * s_len * 4)
    return pl.pallas_call(
        kern,
        grid=(bsz, MLA_HEADS // heads, nt - q_off),
        in_specs=[
            pl.BlockSpec((1, tile, heads * Q_HEAD_PAD), lambda b, h, j: (b, j + q_off, h)),
            pl.BlockSpec((1, s_len, heads * lanes), lambda b, h, j: (b, 0, h)),
            pl.BlockSpec((1, s_len, lanes), lambda b, h, j: (b, 0, 0)),
            pl.BlockSpec((1, s_len, heads * lanes), lambda b, h, j: (b, 0, h)),
            pl.BlockSpec((1, tile, heads * lanes), lambda b, h, j: (b, j + q_off, h)),
        ],
        out_specs=pl.BlockSpec((1, tile, heads * lanes), lambda b, h, j: (b, j + q_off, h)),
        out_shape=jax.ShapeDtypeStruct((bsz, s_len, MLA_DIM), BF16),
        scratch_shapes=[pltpu.VMEM((heads, s_len, Q_HEAD_PAD), BF16)],
        compiler_params=_params(3, vm),
        name="mla_attention",
    )(qp, kn, krt, v, z)


def _even_weights(w_in, gate_b, w_out):
    d = w_in.shape[0]
    n_main = N_PARTS * PART
    ng = w_in.shape[1] - n_main
    w_main = w_in[:, :n_main].reshape(d, N_PARTS, PART).transpose(1, 0, 2).astype(BF16)
    w_gate = jnp.pad(w_in[:, n_main:], ((0, 0), (0, GATE_PAD - ng))).astype(BF16)
    gb = jnp.pad(gate_b, (0, GATE_PAD - ng)).reshape(1, GATE_PAD).astype(F32)
    w_out2 = w_out.reshape(2, PART, w_out.shape[1]).astype(BF16)
    return w_main, w_gate, gb, w_out2


def _swap_halves(w):
    half = w.shape[-1] // 2
    return jnp.concatenate([w[..., half:], w[..., :half]], axis=-1)


def _odd_weights(w_in, w_uq, w_ukv, w_out):
    ql, kvl, rope = Q_LORA_RANK, KV_LORA_RANK, QK_ROPE_DIM
    w_kr = w_in[:, ql + kvl:ql + kvl + rope]
    w_in_x = jnp.concatenate([w_in[:, :ql + kvl], w_kr, _swap_halves(w_kr),
                              w_in[:, ql + kvl + rope:]], axis=1).astype(BF16)
    wq = w_uq.reshape(ql, MLA_HEADS, QK_NOPE_DIM + rope)
    wq_r = wq[:, :, QK_NOPE_DIM:]
    w_uq_x = jnp.concatenate([wq[:, :, :QK_NOPE_DIM], wq_r, _swap_halves(wq_r)], axis=2)
    w_uq_x = w_uq_x.reshape(ql, MLA_HEADS * Q_HEAD_PAD).astype(BF16)
    wkv = w_ukv.reshape(kvl, MLA_HEADS, QK_NOPE_DIM + V_HEAD_DIM)
    w_ukv_x = jnp.concatenate([wkv[:, :, :QK_NOPE_DIM].reshape(kvl, -1),
                               wkv[:, :, QK_NOPE_DIM:].reshape(kvl, -1)], axis=1).astype(BF16)
    w_out2 = w_out.reshape(2, PART, w_out.shape[1]).astype(BF16)
    return w_in_x, w_uq_x, w_ukv_x, w_out2


def _rope_table(t_len, c_len):
    rows = t_len // GRID_W
    row = jnp.repeat(jnp.arange(rows, dtype=F32), GRID_W)
    col = jnp.tile(jnp.arange(GRID_W, dtype=F32), rows)
    n_freq = QK_ROPE_DIM // 4
    inv_freq = ROPE_BASE ** (-jnp.arange(n_freq, dtype=F32) / n_freq)
    ang = jnp.concatenate([row[:, None] * inv_freq, col[:, None] * inv_freq], axis=-1)
    cos, sin = jnp.cos(ang), jnp.sin(ang)
    lat = jnp.concatenate([cos, cos, -sin, sin], axis=-1)
    half = QK_ROPE_DIM
    ctx = jnp.concatenate([jnp.ones((c_len, half), F32), jnp.zeros((c_len, half), F32)], axis=-1)
    return jnp.concatenate([ctx, lat], axis=0)


def kernel(x, c, ctx, c_ctx, mod_w, mod_b, ln_g, ln_b, ev_w_in, ev_conv_w, ev_gate_b, ev_w_out,
           od_w_in, od_q_norm, od_w_uq, od_kv_norm, od_w_ukv, od_w_out):
    bsz, t_len, d = x.shape
    c_len = ctx.shape[1]
    s_len = c_len + t_len
    assert d == PART and c_len % ROW_TILE == 0 and t_len % ROW_TILE == 0
    assert (bsz * s_len) % INPROJ_ROWS == 0 and t_len % GRID_W == 0
    nct = c_len // ROW_TILE

    n_rows = -(-(bsz + 1) // MOD_ROWS_PAD) * MOD_ROWS_PAD
    cc = jnp.concatenate([c, c_ctx[None, :], jnp.zeros((n_rows - bsz - 1, d), F32)], axis=0)
    mods = _modulation(cc, mod_w, mod_b).reshape(DEPTH, n_rows, 1, 3 * d)

    cs = _rope_table(t_len, c_len)
    h, u = _prep(x, ctx, mods[0])

    for layer in range(DEPTH):
        j = layer // 2
        last = layer == DEPTH - 1
        mod_next = None if last else mods[layer + 1]
        lng, lnb = ln_g[layer].reshape(1, d), ln_b[layer].reshape(1, d)
        if layer % 2 == 0:
            w_main, w_gate, gb, w_out2 = _even_weights(ev_w_in[j], ev_gate_b[j], ev_w_out[j])
            main, gates = _even_inproj(u.reshape(bsz * s_len, d), w_main, w_gate, gb)
            main = main.reshape(bsz, s_len, N_PARTS * PART)
            gate_cols, gate_rows = _gate_prep(gates.reshape(bsz, s_len, GATE_PAD))
            mix_b = _mlstm(main, gate_cols, gate_rows, nct)
            mix_a = _conv_branch(main, ev_conv_w[j], nct)
            res = _outproj(mix_a, 0, mix_b, 0, w_out2, h, mods[layer], mod_next, lng, lnb, nct)
        else:
            w_in_x, w_uq_x, w_ukv_x, w_out2 = _odd_weights(od_w_in[j], od_w_uq[j], od_w_ukv[j],
                                                           od_w_out[j])
            qp, kn, v, krt, z = _mla_inproj(u, w_in_x, od_q_norm[j].reshape(1, -1),
                                            od_kv_norm[j].reshape(1, -1), w_uq_x, w_ukv_x, cs)
            og = _mla_attention(qp, kn, krt, v, z, nct, need_ctx=not last)
            res = _outproj(og, 0, og, 1, w_out2, h, mods[layer], mod_next, lng, lnb, nct)
        if last:
            return res
        h, u = res
```

```python
import functools

import jax
import jax.numpy as jnp
from jax import lax
from jax.experimental import pallas as pl
from jax.experimental.pallas import tpu as pltpu

DEPTH = 4
GRID_W = 64
CONV_DIM = 1024
MLSTM_HEADS = 4
MLSTM_HEAD_DIM = 256
MLSTM_DIM = MLSTM_HEADS * MLSTM_HEAD_DIM
MLSTM_K_SCALE = MLSTM_HEAD_DIM ** -0.5
NEG_INIT = -1e30
MLA_HEADS = 16
QK_NOPE_DIM = 128
QK_ROPE_DIM = 64
V_HEAD_DIM = 128
Q_LORA_RANK = 384
KV_LORA_RANK = 256
MLA_DIM = MLA_HEADS * V_HEAD_DIM
MLA_SCALE = (QK_NOPE_DIM + QK_ROPE_DIM) ** -0.5
ROPE_BASE = 10000.0
LN_EPS = 1e-5
RMS_EPS = 1e-6
DEEPNORM_ALPHA = (2 * DEPTH) ** 0.25

V7X_LANES = 128
V7X_MXU_DIM = 256
V7X_BF16_SUBLANES = 16
V7X_VMEM_BYTES = 64 * 1024 * 1024

ROW_TILE = V7X_MXU_DIM
INPROJ_ROWS = 1024
PART = 1024
N_PARTS = 9
GATE_PAD = V7X_LANES
Q_HEAD_PAD = 2 * V7X_LANES
MOD_ROWS_PAD = 8
MLSTM_HEADS_PER_STEP = 2
ATTN_HEADS_PER_STEP = 8
LOG2_E = 1.4426950408889634

F32 = jnp.float32
BF16 = jnp.bfloat16


def _vmem_limit(nbytes):
    return int(min(nbytes + 16 * 1024 * 1024, V7X_VMEM_BYTES - 4 * 1024 * 1024))


def _params(ngrid, nbytes):
    return pltpu.CompilerParams(
        dimension_semantics=("arbitrary",) * ngrid,
        vmem_limit_bytes=_vmem_limit(nbytes))


def _sigmoid(x):
    return 0.5 * jnp.tanh(0.5 * x) + 0.5


def _silu(x):
    return x * _sigmoid(x)


def _dot(a, b):
    return jnp.dot(a, b, preferred_element_type=F32)


def _dot_nt(a, b):
    return lax.dot_general(a, b, (((1,), (1,)), ((), ())), preferred_element_type=F32)


def _mod_kernel(cc_ref, w_ref, b_ref, o_ref):
    cc = cc_ref[...]
    o_ref[0] = _dot(_silu(cc), w_ref[0]) + b_ref[0]


def _modulation(cc, mod_w, mod_b):
    rows, d = cc.shape
    depth, _, n3 = mod_w.shape
    nblk = n3 // d
    return pl.pallas_call(
        _mod_kernel,
        grid=(depth, nblk),
        in_specs=[
            pl.BlockSpec((rows, d), lambda l, n: (0, 0)),
            pl.BlockSpec((1, d, d), lambda l, n: (l, 0, n)),
            pl.BlockSpec((1, 1, d), lambda l, n: (l, 0, n)),
        ],
        out_specs=pl.BlockSpec((1, rows, d), lambda l, n: (l, 0, n)),
        out_shape=jax.ShapeDtypeStruct((depth, rows, n3), F32),
        compiler_params=_params(2, 4 * (rows * d * 4 + d * d * 4)),
        name="modulation",
    )(cc, mod_w, mod_b.reshape(depth, 1, n3))


def _prep_kernel(ctx_ref, x_ref, mod_ref, h_ref, u_ref, *, n_ctx_tiles, d):
    j = pl.program_id(1)
    val = jnp.where(j < n_ctx_tiles, ctx_ref[0], x_ref[0])
    mod = mod_ref[0]
    sh, sc = mod[:, 0:d], mod[:, d:2 * d]
    h_ref[0] = val
    u_ref[0] = (val * (1.0 + sc) + sh).astype(BF16)


def _mod_row_map(n_ctx_tiles, ctx_row):
    return lambda b, j: (jnp.where(j < n_ctx_tiles, ctx_row, b), 0, 0)


def _prep(x, ctx, mod0):
    bsz, t_len, d = x.shape
    c_len = ctx.shape[1]
    s_len = c_len + t_len
    nct, nt = c_len // ROW_TILE, s_len // ROW_TILE
    kern = functools.partial(_prep_kernel, n_ctx_tiles=nct, d=d)
    tile = (1, ROW_TILE, d)
    return pl.pallas_call(
        kern,
        grid=(bsz, nt),
        in_specs=[
            pl.BlockSpec(tile, lambda b, j: (b, jnp.minimum(j, nct - 1), 0)),
            pl.BlockSpec(tile, lambda b, j: (b, jnp.maximum(j - nct, 0), 0)),
            pl.BlockSpec((1, 1, 3 * d), _mod_row_map(nct, bsz)),
        ],
        out_specs=[pl.BlockSpec(tile, lambda b, j: (b, j, 0)),
                   pl.BlockSpec(tile, lambda b, j: (b, j, 0))],
        out_shape=[jax.ShapeDtypeStruct((bsz, s_len, d), F32),
                   jax.ShapeDtypeStruct((bsz, s_len, d), BF16)],
        compiler_params=_params(2, 2 * ROW_TILE * d * (4 + 4 + 4 + 2)),
        name="prep",
    )(ctx, x, mod0)


def _even_inproj_kernel(u_ref, w_ref, wg_ref, gb_ref, main_ref, g_ref):
    n = pl.program_id(1)
    u = u_ref[...]
    main_ref[...] = _dot(u, w_ref[n]).astype(BF16)

    @pl.when(n == 0)
    def _():
        g_ref[...] = _dot(u, wg_ref[...]) + gb_ref[...]


def _even_inproj(u2d, w_main, w_gate, gate_b):
    rows, d = u2d.shape
    tm = INPROJ_ROWS
    vm = (2 * tm * d * 2 + N_PARTS * d * PART * 2 + 2 * tm * PART * 2
          + tm * PART * 4 + 2 * tm * GATE_PAD * 4 + 2 * d * GATE_PAD * 2)
    return pl.pallas_call(
        _even_inproj_kernel,
        grid=(rows // tm, N_PARTS),
        in_specs=[
            pl.BlockSpec((tm, d), lambda i, n: (i, 0)),
            pl.BlockSpec((N_PARTS, d, PART), lambda i, n: (0, 0, 0),
                         pipeline_mode=pl.Buffered(1)),
            pl.BlockSpec((d, GATE_PAD), lambda i, n: (0, 0)),
            pl.BlockSpec((1, GATE_PAD), lambda i, n: (0, 0)),
        ],
        out_specs=[pl.BlockSpec((tm, PART), lambda i, n: (i, n)),
                   pl.BlockSpec((tm, GATE_PAD), lambda i, n: (i, 0))],
        out_shape=[jax.ShapeDtypeStruct((rows, N_PARTS * PART), BF16),
                   jax.ShapeDtypeStruct((rows, GATE_PAD), F32)],
        compiler_params=_params(2, vm),
        name="even_inproj",
    )(u2d, w_main, w_gate, gate_b)


def _log_sigmoid(x):
    return jnp.minimum(x, 0.0) - jnp.log1p(jnp.exp(-jnp.abs(x)))


def _lane_scan(x, op, ident, reverse):
    n = x.shape[1]
    lane = lax.broadcasted_iota(jnp.int32, x.shape, 1)
    k = 1
    while k < n:
        if reverse:
            shifted = jnp.where(lane < n - k, pltpu.roll(x, n - k, 1), ident)
        else:
            shifted = jnp.where(lane >= k, pltpu.roll(x, k, 1), ident)
        x = op(x, shifted)
        k *= 2
    return x


def _gate_prep_kernel(g_ref, c_ref, r_ref, *, n_chunks, chunk):
    nh = MLSTM_HEADS
    assert 2 * nh == 8
    fwd, bwd = [], []
    for c in range(n_chunks):
        gt = g_ref[0, c * chunk:(c + 1) * chunk, :].T
        fwd.append(gt[0:8])
        bwd.append(gt[8:16])

    def scans(g, reverse):
        n_rows = g.shape[0]
        logf = pltpu.roll(_log_sigmoid(g), n_rows - nh, 0)
        b = _lane_scan(logf, jnp.add, 0.0, reverse)
        a = g - b
        return b, a, _lane_scan(a, jnp.maximum, -jnp.inf, reverse)

    kinds = scans(jnp.concatenate(fwd, axis=0), False) + scans(jnp.concatenate(bwd, axis=0), True)
    sub = lax.broadcasted_iota(jnp.int32, (8, chunk), 0)
    pad = jnp.zeros((V7X_LANES - 8, chunk), F32)
    for c in range(n_chunks):
        for h in range(nh):
            row = c * 8 + h
            x8 = jnp.zeros((8, chunk), F32)
            for idx, arr in enumerate(kinds):
                x8 = jnp.where(sub == idx, arr[row:row + 1, :], x8)
            r_ref[0, h, c] = x8
            c_ref[0, h, c * chunk:(c + 1) * chunk, :] = jnp.concatenate([x8, pad], axis=0).T


def _gate_prep(gates):
    bsz, s_len, gp = gates.shape
    chunk = ROW_TILE
    nch = s_len // chunk
    nh = MLSTM_HEADS
    kern = functools.partial(_gate_prep_kernel, n_chunks=nch, chunk=chunk)
    vm = 2 * s_len * gp * 4 * (1 + nh) + 2 * nh * nch * 8 * chunk * 4
    return pl.pallas_call(
        kern,
        grid=(bsz,),
        in_specs=[pl.BlockSpec((1, s_len, gp), lambda b: (b, 0, 0))],
        out_specs=[pl.BlockSpec((1, nh, s_len, V7X_LANES), lambda b: (b, 0, 0, 0)),
                   pl.BlockSpec((1, nh, nch, 8, chunk), lambda b: (b, 0, 0, 0, 0))],
        out_shape=[jax.ShapeDtypeStruct((bsz, nh, s_len, V7X_LANES), F32),
                   jax.ShapeDtypeStruct((bsz, nh, nch, 8, chunk), F32)],
        compiler_params=_params(1, vm),
        name="gate_prep",
    )(gates)


def _mlstm_chunk(q, k, v, cols, a_row, m_state, c_prev, n_prev, direction, mask):
    chunk = q.shape[0]
    lanes = V7X_LANES
    last = chunk - 1 if direction == 0 else 0
    col0 = 3 * direction
    wide = lambda col: jnp.broadcast_to(col, (chunk, lanes))
    twice = lambda x: jnp.concatenate([x, x], axis=1)
    b_w = wide(cols[:, col0:col0 + 1])
    a_w = wide(cols[:, col0 + 1:col0 + 2])
    m_w = jnp.maximum(m_state, wide(cols[:, col0 + 2:col0 + 3]))
    ones = jnp.ones((chunk, lanes), BF16)

    w = jnp.exp(jnp.where(mask, a_row - twice(m_w), -jnp.inf))
    a_inter = jnp.exp(m_state - m_w)
    s = (_dot_nt(q, k) * w).astype(BF16)
    num = _dot(s, v) + twice(a_inter) * _dot(q, c_prev.astype(BF16))
    den = _dot(s, ones) + a_inter * _dot(q, n_prev.astype(BF16))
    h = num * twice(1.0 / jnp.maximum(jnp.abs(den), jnp.exp(-(b_w + m_w))))

    m_last = m_w[last:last + 1, 0:1]
    decay = jnp.exp(m_state - m_last)
    kw_t = (k.astype(F32) * twice(jnp.exp(a_w - m_last))).T.astype(BF16)
    c_new = decay * c_prev + _dot(kw_t, v)
    n_new = decay * n_prev + _dot(kw_t, ones)
    return h, c_new, n_new, b_w[last:last + 1, 0:1] + m_last


def _mlstm_kernel(q_ref, k_ref, v_ref, o_ref, z_ref, c_ref, r_ref, out_ref,
                  hf_ref, hb_ref, cst_ref, nst_ref, *, n_chunks, n_ctx_chunks, chunk, heads):
    dh = MLSTM_HEAD_DIM
    t_idx = lax.broadcasted_iota(jnp.int32, (chunk, chunk), 0)
    s_idx = lax.broadcasted_iota(jnp.int32, (chunk, chunk), 1)
    masks = (s_idx <= t_idx, s_idx >= t_idx)
    h_refs = (hf_ref, hb_ref)
    cst_ref[...] = jnp.zeros_like(cst_ref)
    nst_ref[...] = jnp.zeros_like(nst_ref)

    def body(jj, m_states):
        ci_bwd = jnp.where(jj < n_ctx_chunks, n_ctx_chunks - 1 - jj,
                           n_chunks - 1 - (jj - n_ctx_chunks))
        new_states = []
        for g in range(heads):
            lanes = slice(g * dh, (g + 1) * dh)
            for direction in range(2):
                ci = jj if direction == 0 else ci_bwd
                rows = pl.ds(pl.multiple_of(ci * chunk, chunk), chunk)
                idx = 2 * g + direction
                q = q_ref[0, rows, lanes]
                k = k_ref[0, rows, lanes] * jnp.asarray(MLSTM_K_SCALE, BF16)
                v = v_ref[0, rows, lanes]
                a_row = r_ref[0, g, ci][3 * direction + 1:3 * direction + 2, :]
                h, c_new, n_new, m_new = _mlstm_chunk(
                    q, k, v, c_ref[0, g, rows, :], a_row, m_states[idx],
                    cst_ref[idx], nst_ref[idx], direction, masks[direction])
                cst_ref[idx] = c_new
                nst_ref[idx] = n_new
                h_refs[direction][rows, lanes] = h
                new_states.append(m_new)
        return tuple(new_states)

    init = tuple(jnp.full((1, 1), NEG_INIT, F32) for _ in range(2 * heads))
    lax.fori_loop(0, n_chunks, body, init)

    def gate_body(ci, carry):
        rows = pl.ds(pl.multiple_of(ci * chunk, chunk), chunk)
        og = o_ref[0, rows, :].astype(F32)
        zg = z_ref[0, rows, :].astype(F32)
        out_ref[0, rows, :] = (_sigmoid(og) * (hf_ref[rows, :] + hb_ref[rows, :])
                               * _silu(zg)).astype(BF16)
        return carry

    lax.fori_loop(0, n_chunks, gate_body, 0)


def _mlstm(main, gate_cols, gate_rows, n_ctx_chunks):
    bsz, s_len, _ = main.shape
    nh, dh = MLSTM_HEADS, MLSTM_HEAD_DIM
    heads = MLSTM_HEADS_PER_STEP
    chunk = ROW_TILE
    assert chunk == dh
    nch = s_len // chunk
    width = heads * dh
    blocks_per_part = PART // width

    def part_spec(part):
        return pl.BlockSpec((1, s_len, width), lambda b, h: (b, 0, part * blocks_per_part + h))

    kern = functools.partial(_mlstm_kernel, n_chunks=nch, n_ctx_chunks=n_ctx_chunks,
                             chunk=chunk, heads=heads)
    vm = (2 * 6 * s_len * width * 2 + 2 * heads * s_len * V7X_LANES * 4
          + 2 * heads * nch * 8 * chunk * 4 + 2 * s_len * width * 4
          + 2 * heads * (dh + 1) * dh * 4 + 16 * chunk * chunk * 4)
    return pl.pallas_call(
        kern,
        grid=(bsz, nh // heads),
        in_specs=[part_spec(4), part_spec(5), part_spec(6), part_spec(7), part_spec(8),
                  pl.BlockSpec((1, heads, s_len, V7X_LANES), lambda b, h: (b, h, 0, 0)),
                  pl.BlockSpec((1, heads, nch, 8, chunk), lambda b, h: (b, h, 0, 0, 0))],
        out_specs=pl.BlockSpec((1, s_len, width), lambda b, h: (b, 0, h)),
        out_shape=jax.ShapeDtypeStruct((bsz, s_len, nh * dh), BF16),
        scratch_shapes=[pltpu.VMEM((s_len, width), F32),
                        pltpu.VMEM((s_len, width), F32),
                        pltpu.VMEM((2 * heads, dh, dh), F32),
                        pltpu.VMEM((2 * heads, dh, V7X_LANES), F32)],
        compiler_params=_params(2, vm),
        name="mlstm",
    )(main, main, main, main, main, gate_cols, gate_rows)


def _conv_kernel(ab_ref, ac_ref, ax_ref, az_ref, acp_ref, axp_ref, acn_ref, axn_ref,
                 w_ref, out_ref, *, n_ctx_tiles, n_tiles, tile):
    j = pl.program_id(1)
    starts = jnp.logical_or(j == 0, j == n_ctx_tiles)
    ends = jnp.logical_or(j == n_ctx_tiles - 1, j == n_tiles - 1)
    p = ac_ref[0].astype(F32) * ax_ref[0].astype(F32)
    hb = V7X_BF16_SUBLANES
    prev = (acp_ref[0].astype(F32) * axp_ref[0].astype(F32))[hb - 1:hb, :]
    nxt = (acn_ref[0].astype(F32) * axn_ref[0].astype(F32))[0:1, :]
    prev = prev * jnp.where(starts, 0.0, 1.0)
    nxt = nxt * jnp.where(ends, 0.0, 1.0)
    row = lax.broadcasted_iota(jnp.int32, p.shape, 0)
    p_dn = jnp.where(row == 0, prev, pltpu.roll(p, 1, 0))
    p_up = jnp.where(row == tile - 1, nxt, pltpu.roll(p, tile - 1, 0))
    w = w_ref[...]
    conv = w[0:1, :] * p_dn + w[1:2, :] * p + w[2:3, :] * p_up
    out_ref[0] = (ab_ref[0].astype(F32) * conv * _silu(az_ref[0].astype(F32))).astype(BF16)


def _conv_branch(main, conv_w, n_ctx_tiles):
    bsz, s_len, _ = main.shape
    tile = ROW_TILE
    nt = s_len // tile
    hb = V7X_BF16_SUBLANES
    per = tile // hb
    n_halo = s_len // hb

    def part_spec(part):
        return pl.BlockSpec((1, tile, PART), lambda b, j: (b, j, part))

    def prev_spec(part):
        return pl.BlockSpec((1, hb, PART), lambda b, j: (b, jnp.maximum(j * per - 1, 0), part))

    def next_spec(part):
        return pl.BlockSpec((1, hb, PART),
                            lambda b, j: (b, jnp.minimum((j + 1) * per, n_halo - 1), part))

    kern = functools.partial(_conv_kernel, n_ctx_tiles=n_ctx_tiles, n_tiles=nt, tile=tile)
    vm = 2 * 5 * tile * PART * 2 + 8 * tile * PART * 4
    return pl.pallas_call(
        kern,
        grid=(bsz, nt),
        in_specs=[part_spec(0), part_spec(1), part_spec(2), part_spec(3),
                  prev_spec(1), prev_spec(2), next_spec(1), next_spec(2),
                  pl.BlockSpec(conv_w.shape, lambda b, j: (0, 0))],
        out_specs=pl.BlockSpec((1, tile, PART), lambda b, j: (b, j, 0)),
        out_shape=jax.ShapeDtypeStruct((bsz, s_len, PART), BF16),
        compiler_params=_params(2, vm),
        name="conv_branch",
    )(main, main, main, main, main, main, main, main, conv_w)


def _outproj_kernel(*refs, d, has_next):
    if has_next:
        a1_ref, a2_ref, w_ref, h_ref, modc_ref, lng_ref, lnb_ref, modn_ref, hn_ref, u_ref = refs
    else:
        a1_ref, a2_ref, w_ref, h_ref, modc_ref, lng_ref, lnb_ref, hn_ref = refs
    y = _dot(a1_ref[0], w_ref[0]) + _dot(a2_ref[0], w_ref[1])
    gate = modc_ref[0][:, 2 * d:3 * d]
    r = DEEPNORM_ALPHA * h_ref[0] + gate * y
    mu = jnp.mean(r, axis=-1, keepdims=True)
    xc = r - mu
    var = jnp.mean(xc * xc, axis=-1, keepdims=True)
    hn = xc * lax.rsqrt(var + LN_EPS) * lng_ref[...] + lnb_ref[...]
    hn_ref[0] = hn
    if has_next:
        modn = modn_ref[0]
        u_ref[0] = (hn * (1.0 + modn[:, d:2 * d]) + modn[:, 0:d]).astype(BF16)


def _outproj(a1_arr, a1_blk, a2_arr, a2_blk, w2, h, mod_cur, mod_next, ln_g, ln_b, n_ctx_tiles):
    bsz, s_len, d = h.shape
    tile = ROW_TILE
    nt = s_len // tile
    has_next = mod_next is not None
    off = 0 if has_next else n_ctx_tiles
    mod_map = _mod_row_map(n_ctx_tiles - off, bsz)
    row_blk = lambda b, j: (b, j + off, 0)
    in_specs = [
        pl.BlockSpec((1, tile, PART), lambda b, j: (b, j + off, a1_blk)),
        pl.BlockSpec((1, tile, PART), lambda b, j: (b, j + off, a2_blk)),
        pl.BlockSpec(w2.shape, lambda b, j: (0, 0, 0)),
        pl.BlockSpec((1, tile, d), row_blk),
        pl.BlockSpec((1, 1, 3 * d), mod_map),
        pl.BlockSpec((1, d), lambda b, j: (0, 0)),
        pl.BlockSpec((1, d), lambda b, j: (0, 0)),
    ]
    args = [a1_arr, a2_arr, w2, h, mod_cur, ln_g, ln_b]
    out_specs = [pl.BlockSpec((1, tile, d), lambda b, j: (b, j, 0))]
    out_shape = [jax.ShapeDtypeStruct((bsz, s_len - off * tile, d), F32)]
    if has_next:
        in_specs.append(pl.BlockSpec((1, 1, 3 * d), mod_map))
        args.append(mod_next)
        out_specs.append(pl.BlockSpec((1, tile, d), lambda b, j: (b, j, 0)))
        out_shape.append(jax.ShapeDtypeStruct((bsz, s_len, d), BF16))
    kern = functools.partial(_outproj_kernel, d=d, has_next=has_next)
    vm = 2 * (2 * tile * PART * 2 + 2 * PART * d * 2 + tile * d * (4 + 4 + 2)) + 6 * tile * d * 4
    res = pl.pallas_call(
        kern,
        grid=(bsz, nt - off),
        in_specs=in_specs,
        out_specs=out_specs,
        out_shape=out_shape,
        compiler_params=_params(2, vm),
        name="outproj_norm",
    )(*args)
    return res if has_next else res[0]


def _rope_pair(x, cs):
    prod = x * cs
    return prod + pltpu.roll(prod, QK_ROPE_DIM, 1)


def _mla_inproj_kernel(u_ref, win_ref, qn_ref, kvn_ref, wuq_ref, wukv_ref, cs_ref,
                       qp_ref, kp_ref, va_ref, z_ref):
    ql, kvl, lanes = Q_LORA_RANK, KV_LORA_RANK, V7X_LANES
    t = _dot(u_ref[0], win_ref[...])
    q_c = t[:, 0:ql]
    kv_c = t[:, ql:ql + kvl]
    kr2 = t[:, ql + kvl:ql + kvl + lanes]
    z_ref[0] = t[:, ql + kvl + lanes:].astype(BF16)
    cs = cs_ref[...]

    lane = lax.broadcasted_iota(jnp.int32, kr2.shape, 1)
    krt = jnp.where(lane < QK_ROPE_DIM, _rope_pair(kr2, cs), 0.0).astype(BF16)
    ones = jnp.ones(kr2.shape, BF16)

    qn = q_c * lax.rsqrt(jnp.mean(q_c * q_c, axis=-1, keepdims=True) + RMS_EPS) * qn_ref[...]
    q = _dot(qn.astype(BF16), wuq_ref[...])
    q_scale = MLA_SCALE * LOG2_E
    kvn = kv_c * lax.rsqrt(jnp.mean(kv_c * kv_c, axis=-1, keepdims=True) + RMS_EPS) * kvn_ref[...]
    kv = _dot(kvn.astype(BF16), wukv_ref[...])
    half = MLA_HEADS * QK_NOPE_DIM
    for h in range(MLA_HEADS):
        base = h * Q_HEAD_PAD
        qp_ref[0, :, base:base + lanes] = (q[:, base:base + lanes] * q_scale).astype(BF16)
        qp_ref[0, :, base + lanes:base + 2 * lanes] = (
            _rope_pair(q[:, base + lanes:base + 2 * lanes], cs) * q_scale).astype(BF16)
        kp_ref[0, :, base:base + lanes] = kv[:, h * lanes:(h + 1) * lanes].astype(BF16)
        kp_ref[0, :, base + lanes:base + 2 * lanes] = krt
        va_ref[0, :, base:base + lanes] = kv[:, half + h * lanes:half + (h + 1) * lanes].astype(BF16)
        va_ref[0, :, base + lanes:base + 2 * lanes] = ones


def _mla_inproj(u, w_in, q_norm, kv_norm, w_uq, w_ukv, cs):
    bsz, s_len, d = u.shape
    tile = ROW_TILE
    nt = s_len // tile
    n_in = w_in.shape[1]
    nq = w_uq.shape[1]
    nkv = w_ukv.shape[1]
    zdim = MLA_DIM
    full = lambda a: pl.BlockSpec(a.shape, lambda b, j: (0,) * a.ndim)
    row = lambda w: pl.BlockSpec((1, tile, w), lambda b, j: (b, j, 0))
    vm = (2 * (tile * d * 2 + d * n_in * 2 + w_uq.size * 2 + w_ukv.size * 2
               + tile * (3 * nq + zdim) * 2)
          + tile * (n_in + nq + nkv) * 4 * 2)
    return pl.pallas_call(
        _mla_inproj_kernel,
        grid=(bsz, nt),
        in_specs=[row(d), full(w_in), full(q_norm), full(kv_norm), full(w_uq), full(w_ukv),
                  pl.BlockSpec((tile, V7X_LANES), lambda b, j: (j, 0))],
        out_specs=[row(nq), row(nq), row(nq), row(zdim)],
        out_shape=[jax.ShapeDtypeStruct((bsz, s_len, nq), BF16),
                   jax.ShapeDtypeStruct((bsz, s_len, nq), BF16),
                   jax.ShapeDtypeStruct((bsz, s_len, nq), BF16),
                   jax.ShapeDtypeStruct((bsz, s_len, zdim), BF16)],
        compiler_params=_params(2, vm),
        name="mla_inproj",
    )(u, w_in, q_norm, kv_norm, w_uq, w_ukv, cs)


def _mla_attn_kernel(q_ref, kp_ref, va_ref, z_ref, o_ref,
                     *, q_off, n_ctx_tiles, c_len, s_len, heads):
    jq = pl.program_id(2)
    lanes = V7X_LANES

    def scores(g, n_keys):
        blk = slice(g * Q_HEAD_PAD, (g + 1) * Q_HEAD_PAD)
        return _dot_nt(q_ref[0, :, blk], kp_ref[0, 0:n_keys, blk])

    def softmax(s):
        return jnp.exp2(s - jnp.max(s, axis=1, keepdims=True)).astype(BF16)

    def values(g, p, n_keys):
        oa = _dot(p, va_ref[0, 0:n_keys, g * Q_HEAD_PAD:(g + 1) * Q_HEAD_PAD])
        o = oa[:, 0:lanes] / oa[:, lanes:2 * lanes]
        zg = z_ref[0, :, g * lanes:(g + 1) * lanes].astype(F32)
        o_ref[0, :, g * lanes:(g + 1) * lanes] = (o * _silu(zg)).astype(BF16)

    def attend(n_keys):
        s = {0: scores(0, n_keys)}
        p = {}
        for g in range(heads + 1):
            if g + 1 < heads:
                s[g + 1] = scores(g + 1, n_keys)
            if g < heads:
                p[g] = softmax(s.pop(g))
            if g >= 1:
                values(g - 1, p.pop(g - 1), n_keys)

    if q_off < n_ctx_tiles:
        @pl.when(jq + q_off < n_ctx_tiles)
        def _():
            attend(c_len)

        @pl.when(jq + q_off >= n_ctx_tiles)
        def _():
            attend(s_len)
    else:
        attend(s_len)


def _mla_attention(qp, kp, va, z, n_ctx_tiles, need_ctx):
    bsz, s_len, _ = qp.shape
    tile = ROW_TILE
    nt = s_len // tile
    q_off = 0 if need_ctx else n_ctx_tiles
    heads = ATTN_HEADS_PER_STEP
    kern = functools.partial(_mla_attn_kernel, q_off=q_off, n_ctx_tiles=n_ctx_tiles,
                             c_len=n_ctx_tiles * tile, s_len=s_len, heads=heads)
    lanes = V7X_LANES
    wide = heads * Q_HEAD_PAD
    vm = (2 * (tile * wide * 2 + 2 * s_len * wide * 2 + 2 * tile * heads * lanes * 2)
          + 3 * tile * s_len * (4 + 2))
    return pl.pallas_call(
        kern,
        grid=(bsz, MLA_HEADS // heads, nt - q_off),
        in_specs=[
            pl.BlockSpec((1, tile, wide), lambda b, h, j: (b, j + q_off, h)),
            pl.BlockSpec((1, s_len, wide), lambda b, h, j: (b, 0, h)),
            pl.BlockSpec((1, s_len, wide), lambda b, h, j: (b, 0, h)),
            pl.BlockSpec((1, tile, heads * lanes), lambda b, h, j: (b, j + q_off, h)),
        ],
        out_specs=pl.BlockSpec((1, tile, heads * lanes), lambda b, h, j: (b, j + q_off, h)),
        out_shape=jax.ShapeDtypeStruct((bsz, s_len, MLA_DIM), BF16),
        compiler_params=_params(3, vm),
        name="mla_attention",
    )(qp, kp, va, z)


def _even_weights(w_in, gate_b, w_out):
    d = w_in.shape[0]
    n_main = N_PARTS * PART
    ng = w_in.shape[1] - n_main
    w_main = w_in[:, :n_main].reshape(d, N_PARTS, PART).transpose(1, 0, 2).astype(BF16)
    w_gate = jnp.pad(w_in[:, n_main:], ((0, 0), (0, GATE_PAD - ng))).astype(BF16)
    gb = jnp.pad(gate_b, (0, GATE_PAD - ng)).reshape(1, GATE_PAD).astype(F32)
    w_out2 = w_out.reshape(2, PART, w_out.shape[1]).astype(BF16)
    return w_main, w_gate, gb, w_out2


def _swap_halves(w):
    half = w.shape[-1] // 2
    return jnp.concatenate([w[..., half:], w[..., :half]], axis=-1)


def _odd_weights(w_in, w_uq, w_ukv, w_out):
    ql, kvl, rope = Q_LORA_RANK, KV_LORA_RANK, QK_ROPE_DIM
    w_kr = w_in[:, ql + kvl:ql + kvl + rope]
    w_in_x = jnp.concatenate([w_in[:, :ql + kvl], w_kr, _swap_halves(w_kr),
                              w_in[:, ql + kvl + rope:]], axis=1).astype(BF16)
    wq = w_uq.reshape(ql, MLA_HEADS, QK_NOPE_DIM + rope)
    wq_r = wq[:, :, QK_NOPE_DIM:]
    w_uq_x = jnp.concatenate([wq[:, :, :QK_NOPE_DIM], wq_r, _swap_halves(wq_r)], axis=2)
    w_uq_x = w_uq_x.reshape(ql, MLA_HEADS * Q_HEAD_PAD).astype(BF16)
    wkv = w_ukv.reshape(kvl, MLA_HEADS, QK_NOPE_DIM + V_HEAD_DIM)
    w_ukv_x = jnp.concatenate([wkv[:, :, :QK_NOPE_DIM].reshape(kvl, -1),
                               wkv[:, :, QK_NOPE_DIM:].reshape(kvl, -1)], axis=1).astype(BF16)
    w_out2 = w_out.reshape(2, PART, w_out.shape[1]).astype(BF16)
    return w_in_x, w_uq_x, w_ukv_x, w_out2


def _rope_table(t_len, c_len):
    rows = t_len // GRID_W
    row = jnp.repeat(jnp.arange(rows, dtype=F32), GRID_W)
    col = jnp.tile(jnp.arange(GRID_W, dtype=F32), rows)
    n_freq = QK_ROPE_DIM // 4
    inv_freq = ROPE_BASE ** (-jnp.arange(n_freq, dtype=F32) / n_freq)
    ang = jnp.concatenate([row[:, None] * inv_freq, col[:, None] * inv_freq], axis=-1)
    cos, sin = jnp.cos(ang), jnp.sin(ang)
    lat = jnp.concatenate([cos, cos, -sin, sin], axis=-1)
    half = QK_ROPE_DIM
    ctx = jnp.concatenate([jnp.ones((c_len, half), F32), jnp.zeros((c_len, half), F32)], axis=-1)
    return jnp.concatenate([ctx, lat], axis=0)


def kernel(x, c, ctx, c_ctx, mod_w, mod_b, ln_g, ln_b, ev_w_in, ev_conv_w, ev_gate_b, ev_w_out,
           od_w_in, od_q_norm, od_w_uq, od_kv_norm, od_w_ukv, od_w_out):
    bsz, t_len, d = x.shape
    c_len = ctx.shape[1]
    s_len = c_len + t_len
    assert d == PART and c_len % ROW_TILE == 0 and t_len % ROW_TILE == 0
    assert (bsz * s_len) % INPROJ_ROWS == 0 and t_len % GRID_W == 0
    nct = c_len // ROW_TILE

    n_rows = -(-(bsz + 1) // MOD_ROWS_PAD) * MOD_ROWS_PAD
    cc = jnp.concatenate([c, c_ctx[None, :], jnp.zeros((n_rows - bsz - 1, d), F32)], axis=0)
    mods = _modulation(cc, mod_w, mod_b).reshape(DEPTH, n_rows, 1, 3 * d)

    cs = _rope_table(t_len, c_len)
    h, u = _prep(x, ctx, mods[0])

    for layer in range(DEPTH):
        j = layer // 2
        last = layer == DEPTH - 1
        mod_next = None if last else mods[layer + 1]
        lng, lnb = ln_g[layer].reshape(1, d), ln_b[layer].reshape(1, d)
        if layer % 2 == 0:
            w_main, w_gate, gb, w_out2 = _even_weights(ev_w_in[j], ev_gate_b[j], ev_w_out[j])
            main, gates = _even_inproj(u.reshape(bsz * s_len, d), w_main, w_gate, gb)
            main = main.reshape(bsz, s_len, N_PARTS * PART)
            gate_cols, gate_rows = _gate_prep(gates.reshape(bsz, s_len, GATE_PAD))
            mix_b = _mlstm(main, gate_cols, gate_rows, nct)
            mix_a = _conv_branch(main, ev_conv_w[j], nct)
            res = _outproj(mix_a, 0, mix_b, 0, w_out2, h, mods[layer], mod_next, lng, lnb, nct)
        else:
            w_in_x, w_uq_x, w_ukv_x, w_out2 = _odd_weights(od_w_in[j], od_w_uq[j], od_w_ukv[j],
                                                           od_w_out[j])
            qp, kp, va, z = _mla_inproj(u, w_in_x, od_q_norm[j].reshape(1, -1),
                                        od_kv_norm[j].reshape(1, -1), w_uq_x, w_ukv_x, cs)
            og = _mla_attention(qp, kp, va, z, nct, need_ctx=not last)
            res = _outproj(og, 0, og, 1, w_out2, h, mods[layer], mod_next, lng, lnb, nct)
        if last:
            return res
        h, u = res
```

```python
import functools

import jax
import jax.numpy as jnp
from jax import lax
from jax.experimental import pallas as pl
from jax.experimental.pallas import tpu as pltpu

DEPTH = 4
GRID_W = 64
CONV_DIM = 1024
MLSTM_HEADS = 4
MLSTM_HEAD_DIM = 256
MLSTM_DIM = MLSTM_HEADS * MLSTM_HEAD_DIM
MLSTM_K_SCALE = MLSTM_HEAD_DIM ** -0.5
NEG_INIT = -1e30
MLA_HEADS = 16
QK_NOPE_DIM = 128
QK_ROPE_DIM = 64
V_HEAD_DIM = 128
Q_LORA_RANK = 384
KV_LORA_RANK = 256
MLA_DIM = MLA_HEADS * V_HEAD_DIM
MLA_SCALE = (QK_NOPE_DIM + QK_ROPE_DIM) ** -0.5
ROPE_BASE = 10000.0
LN_EPS = 1e-5
RMS_EPS = 1e-6
DEEPNORM_ALPHA = (2 * DEPTH) ** 0.25

V7X_LANES = 128
V7X_MXU_DIM = 256
V7X_BF16_SUBLANES = 16
V7X_VMEM_BYTES = 64 * 1024 * 1024

ROW_TILE = V7X_MXU_DIM
INPROJ_ROWS = 1024
PART = 1024
N_PARTS = 9
N_PAIRED = 2
CONV_G_PART, CONV_P_PART = 0, 1
MLSTM_Q_PART = 2
EVEN_PART_ORDER = (0, 3, 1, 2, 4, 5, 6, 7, 8)
GATE_PAD = V7X_LANES
Q_HEAD_PAD = 2 * V7X_LANES
MOD_ROWS_PAD = 8
MLSTM_HEADS_PER_STEP = 2
ATTN_HEADS_PER_STEP = 8
LOG2_E = 1.4426950408889634

F32 = jnp.float32
BF16 = jnp.bfloat16


def _vmem_limit(nbytes):
    return int(min(nbytes + 16 * 1024 * 1024, V7X_VMEM_BYTES - 4 * 1024 * 1024))


def _params(ngrid, nbytes):
    return pltpu.CompilerParams(
        dimension_semantics=("arbitrary",) * ngrid,
        vmem_limit_bytes=_vmem_limit(nbytes))


def _sigmoid(x):
    return 0.5 * jnp.tanh(0.5 * x) + 0.5


def _silu(x):
    return x * _sigmoid(x)


def _dot(a, b):
    return jnp.dot(a, b, preferred_element_type=F32)


def _dot_nt(a, b):
    return lax.dot_general(a, b, (((1,), (1,)), ((), ())), preferred_element_type=F32)


def _mod_kernel(cc_ref, w_ref, b_ref, o_ref):
    cc = cc_ref[...]
    o_ref[0] = _dot(_silu(cc), w_ref[0]) + b_ref[0]


def _modulation(cc, mod_w, mod_b):
    rows, d = cc.shape
    depth, _, n3 = mod_w.shape
    nblk = n3 // d
    return pl.pallas_call(
        _mod_kernel,
        grid=(depth, nblk),
        in_specs=[
            pl.BlockSpec((rows, d), lambda l, n: (0, 0)),
            pl.BlockSpec((1, d, d), lambda l, n: (l, 0, n)),
            pl.BlockSpec((1, 1, d), lambda l, n: (l, 0, n)),
        ],
        out_specs=pl.BlockSpec((1, rows, d), lambda l, n: (l, 0, n)),
        out_shape=jax.ShapeDtypeStruct((depth, rows, n3), F32),
        compiler_params=_params(2, 4 * (rows * d * 4 + d * d * 4)),
        name="modulation",
    )(cc, mod_w, mod_b.reshape(depth, 1, n3))


def _prep_kernel(ctx_ref, x_ref, mod_ref, h_ref, u_ref, *, n_ctx_tiles, d):
    j = pl.program_id(1)
    val = jnp.where(j < n_ctx_tiles, ctx_ref[0], x_ref[0])
    mod = mod_ref[0]
    sh, sc = mod[:, 0:d], mod[:, d:2 * d]
    h_ref[0] = val
    u_ref[0] = (val * (1.0 + sc) + sh).astype(BF16)


def _mod_row_map(n_ctx_tiles, ctx_row):
    return lambda b, j: (jnp.where(j < n_ctx_tiles, ctx_row, b), 0, 0)


def _prep(x, ctx, mod0):
    bsz, t_len, d = x.shape
    c_len = ctx.shape[1]
    s_len = c_len + t_len
    nct, nt = c_len // ROW_TILE, s_len // ROW_TILE
    kern = functools.partial(_prep_kernel, n_ctx_tiles=nct, d=d)
    tile = (1, ROW_TILE, d)
    return pl.pallas_call(
        kern,
        grid=(bsz, nt),
        in_specs=[
            pl.BlockSpec(tile, lambda b, j: (b, jnp.minimum(j, nct - 1), 0)),
            pl.BlockSpec(tile, lambda b, j: (b, jnp.maximum(j - nct, 0), 0)),
            pl.BlockSpec((1, 1, 3 * d), _mod_row_map(nct, bsz)),
        ],
        out_specs=[pl.BlockSpec(tile, lambda b, j: (b, j, 0)),
                   pl.BlockSpec(tile, lambda b, j: (b, j, 0))],
        out_shape=[jax.ShapeDtypeStruct((bsz, s_len, d), F32),
                   jax.ShapeDtypeStruct((bsz, s_len, d), BF16)],
        compiler_params=_params(2, 2 * ROW_TILE * d * (4 + 4 + 4 + 2)),
        name="prep",
    )(ctx, x, mod0)


def _even_inproj_kernel(u_ref, w_ref, wg_ref, gb_ref, main_ref, g_ref):
    n = pl.program_id(1)
    u = u_ref[...]

    @pl.when(n == 0)
    def _():
        main_ref[...] = (_dot(u, w_ref[0]) * _silu(_dot(u, w_ref[1]))).astype(BF16)
        g_ref[...] = _dot(u, wg_ref[...]) + gb_ref[...]

    @pl.when(n == 1)
    def _():
        main_ref[...] = (_dot(u, w_ref[2]) * _dot(u, w_ref[3])).astype(BF16)

    @pl.when(n >= N_PAIRED)
    def _():
        main_ref[...] = _dot(u, w_ref[n + N_PAIRED]).astype(BF16)


def _even_inproj(u2d, w_main, w_gate, gate_b):
    rows, d = u2d.shape
    tm = INPROJ_ROWS
    n_out = N_PARTS - N_PAIRED
    vm = (2 * tm * d * 2 + N_PARTS * d * PART * 2 + 2 * tm * PART * 2
          + 3 * tm * PART * 4 + 2 * tm * GATE_PAD * 4 + 2 * d * GATE_PAD * 2)
    return pl.pallas_call(
        _even_inproj_kernel,
        grid=(rows // tm, n_out),
        in_specs=[
            pl.BlockSpec((tm, d), lambda i, n: (i, 0)),
            pl.BlockSpec((N_PARTS, d, PART), lambda i, n: (0, 0, 0),
                         pipeline_mode=pl.Buffered(1)),
            pl.BlockSpec((d, GATE_PAD), lambda i, n: (0, 0)),
            pl.BlockSpec((1, GATE_PAD), lambda i, n: (0, 0)),
        ],
        out_specs=[pl.BlockSpec((tm, PART), lambda i, n: (i, n)),
                   pl.BlockSpec((tm, GATE_PAD), lambda i, n: (i, 0))],
        out_shape=[jax.ShapeDtypeStruct((rows, n_out * PART), BF16),
                   jax.ShapeDtypeStruct((rows, GATE_PAD), F32)],
        compiler_params=_params(2, vm),
        name="even_inproj",
    )(u2d, w_main, w_gate, gate_b)


def _log_sigmoid(x):
    return jnp.minimum(x, 0.0) - jnp.log1p(jnp.exp(-jnp.abs(x)))


def _lane_scan(x, op, ident, reverse):
    n = x.shape[1]
    lane = lax.broadcasted_iota(jnp.int32, x.shape, 1)
    k = 1
    while k < n:
        if reverse:
            shifted = jnp.where(lane < n - k, pltpu.roll(x, n - k, 1), ident)
        else:
            shifted = jnp.where(lane >= k, pltpu.roll(x, k, 1), ident)
        x = op(x, shifted)
        k *= 2
    return x


def _gate_prep_kernel(g_ref, c_ref, r_ref, *, n_chunks, chunk):
    nh = MLSTM_HEADS
    assert 2 * nh == 8
    fwd, bwd = [], []
    for c in range(n_chunks):
        gt = g_ref[0, c * chunk:(c + 1) * chunk, :].T
        fwd.append(gt[0:8])
        bwd.append(gt[8:16])

    def scans(g, reverse):
        n_rows = g.shape[0]
        logf = pltpu.roll(_log_sigmoid(g), n_rows - nh, 0)
        b = _lane_scan(logf, jnp.add, 0.0, reverse)
        a = g - b
        return b, a, _lane_scan(a, jnp.maximum, -jnp.inf, reverse)

    kinds = scans(jnp.concatenate(fwd, axis=0), False) + scans(jnp.concatenate(bwd, axis=0), True)
    sub = lax.broadcasted_iota(jnp.int32, (8, chunk), 0)
    pad = jnp.zeros((V7X_LANES - 8, chunk), F32)
    for c in range(n_chunks):
        for h in range(nh):
            row = c * 8 + h
            x8 = jnp.zeros((8, chunk), F32)
            for idx, arr in enumerate(kinds):
                x8 = jnp.where(sub == idx, arr[row:row + 1, :], x8)
            r_ref[0, h, c] = x8
            c_ref[0, h, c * chunk:(c + 1) * chunk, :] = jnp.concatenate([x8, pad], axis=0).T


def _gate_prep(gates):
    bsz, s_len, gp = gates.shape
    chunk = ROW_TILE
    nch = s_len // chunk
    nh = MLSTM_HEADS
    kern = functools.partial(_gate_prep_kernel, n_chunks=nch, chunk=chunk)
    vm = 2 * s_len * gp * 4 * (1 + nh) + 2 * nh * nch * 8 * chunk * 4
    return pl.pallas_call(
        kern,
        grid=(bsz,),
        in_specs=[pl.BlockSpec((1, s_len, gp), lambda b: (b, 0, 0))],
        out_specs=[pl.BlockSpec((1, nh, s_len, V7X_LANES), lambda b: (b, 0, 0, 0)),
                   pl.BlockSpec((1, nh, nch, 8, chunk), lambda b: (b, 0, 0, 0, 0))],
        out_shape=[jax.ShapeDtypeStruct((bsz, nh, s_len, V7X_LANES), F32),
                   jax.ShapeDtypeStruct((bsz, nh, nch, 8, chunk), F32)],
        compiler_params=_params(1, vm),
        name="gate_prep",
    )(gates)


def _mlstm_chunk(q, k, v, cols, a_row, m_state, c_prev, n_prev, direction, mask):
    chunk = q.shape[0]
    lanes = V7X_LANES
    last = chunk - 1 if direction == 0 else 0
    col0 = 3 * direction
    wide = lambda col: jnp.broadcast_to(col, (chunk, lanes))
    twice = lambda x: jnp.concatenate([x, x], axis=1)
    b_w = wide(cols[:, col0:col0 + 1])
    a_w = wide(cols[:, col0 + 1:col0 + 2])
    m_w = jnp.maximum(m_state, wide(cols[:, col0 + 2:col0 + 3]))
    ones = jnp.ones((chunk, lanes), BF16)

    w = jnp.exp(jnp.where(mask, a_row - twice(m_w), -jnp.inf))
    a_inter = jnp.exp(m_state - m_w)
    s = (_dot_nt(q, k) * w).astype(BF16)
    num = _dot(s, v) + twice(a_inter) * _dot(q, c_prev.astype(BF16))
    den = _dot(s, ones) + a_inter * _dot(q, n_prev.astype(BF16))
    h = num * twice(1.0 / jnp.maximum(jnp.abs(den), jnp.exp(-(b_w + m_w))))

    m_last = m_w[last:last + 1, 0:1]
    decay = jnp.exp(m_state - m_last)
    kw_t = (k.astype(F32) * twice(jnp.exp(a_w - m_last))).T.astype(BF16)
    c_new = decay * c_prev + _dot(kw_t, v)
    n_new = decay * n_prev + _dot(kw_t, ones)
    return h, c_new, n_new, b_w[last:last + 1, 0:1] + m_last


def _mlstm_kernel(q_ref, k_ref, v_ref, o_ref, z_ref, c_ref, r_ref, out_ref,
                  hf_ref, hb_ref, cst_ref, nst_ref, *, n_chunks, n_ctx_chunks, chunk, heads):
    dh = MLSTM_HEAD_DIM
    t_idx = lax.broadcasted_iota(jnp.int32, (chunk, chunk), 0)
    s_idx = lax.broadcasted_iota(jnp.int32, (chunk, chunk), 1)
    masks = (s_idx <= t_idx, s_idx >= t_idx)
    h_refs = (hf_ref, hb_ref)
    cst_ref[...] = jnp.zeros_like(cst_ref)
    nst_ref[...] = jnp.zeros_like(nst_ref)

    def body(jj, m_states):
        ci_bwd = jnp.where(jj < n_ctx_chunks, n_ctx_chunks - 1 - jj,
                           n_chunks - 1 - (jj - n_ctx_chunks))
        new_states = []
        for g in range(heads):
            lanes = slice(g * dh, (g + 1) * dh)
            for direction in range(2):
                ci = jj if direction == 0 else ci_bwd
                rows = pl.ds(pl.multiple_of(ci * chunk, chunk), chunk)
                idx = 2 * g + direction
                q = q_ref[0, rows, lanes]
                k = k_ref[0, rows, lanes] * jnp.asarray(MLSTM_K_SCALE, BF16)
                v = v_ref[0, rows, lanes]
                a_row = r_ref[0, g, ci][3 * direction + 1:3 * direction + 2, :]
                h, c_new, n_new, m_new = _mlstm_chunk(
                    q, k, v, c_ref[0, g, rows, :], a_row, m_states[idx],
                    cst_ref[idx], nst_ref[idx], direction, masks[direction])
                cst_ref[idx] = c_new
                nst_ref[idx] = n_new
                h_refs[direction][rows, lanes] = h
                new_states.append(m_new)
        return tuple(new_states)

    init = tuple(jnp.full((1, 1), NEG_INIT, F32) for _ in range(2 * heads))
    lax.fori_loop(0, n_chunks, body, init)

    def gate_body(ci, carry):
        rows = pl.ds(pl.multiple_of(ci * chunk, chunk), chunk)
        og = o_ref[0, rows, :].astype(F32)
        zg = z_ref[0, rows, :].astype(F32)
        out_ref[0, rows, :] = (_sigmoid(og) * (hf_ref[rows, :] + hb_ref[rows, :])
                               * _silu(zg)).astype(BF16)
        return carry

    lax.fori_loop(0, n_chunks, gate_body, 0)


def _mlstm(main, gate_cols, gate_rows, n_ctx_chunks):
    bsz, s_len, _ = main.shape
    nh, dh = MLSTM_HEADS, MLSTM_HEAD_DIM
    heads = MLSTM_HEADS_PER_STEP
    chunk = ROW_TILE
    assert chunk == dh
    nch = s_len // chunk
    width = heads * dh
    blocks_per_part = PART // width

    def part_spec(part):
        return pl.BlockSpec((1, s_len, width), lambda b, h: (b, 0, part * blocks_per_part + h))

    kern = functools.partial(_mlstm_kernel, n_chunks=nch, n_ctx_chunks=n_ctx_chunks,
                             chunk=chunk, heads=heads)
    vm = (2 * 6 * s_len * width * 2 + 2 * heads * s_len * V7X_LANES * 4
          + 2 * heads * nch * 8 * chunk * 4 + 2 * s_len * width * 4
          + 2 * heads * (dh + 1) * dh * 4 + 16 * chunk * chunk * 4)
    return pl.pallas_call(
        kern,
        grid=(bsz, nh // heads),
        in_specs=[part_spec(MLSTM_Q_PART + i) for i in range(5)] + [
                  pl.BlockSpec((1, heads, s_len, V7X_LANES), lambda b, h: (b, h, 0, 0)),
                  pl.BlockSpec((1, heads, nch, 8, chunk), lambda b, h: (b, h, 0, 0, 0))],
        out_specs=pl.BlockSpec((1, s_len, width), lambda b, h: (b, 0, h)),
        out_shape=jax.ShapeDtypeStruct((bsz, s_len, nh * dh), BF16),
        scratch_shapes=[pltpu.VMEM((s_len, width), F32),
                        pltpu.VMEM((s_len, width), F32),
                        pltpu.VMEM((2 * heads, dh, dh), F32),
                        pltpu.VMEM((2 * heads, dh, V7X_LANES), F32)],
        compiler_params=_params(2, vm),
        name="mlstm",
    )(main, main, main, main, main, gate_cols, gate_rows)


def _conv_mix(g_ref, p_ref, p_prev_ref, p_next_ref, w_ref, j, n_ctx_tiles, n_tiles):
    starts = jnp.logical_or(j == 0, j == n_ctx_tiles)
    ends = jnp.logical_or(j == n_ctx_tiles - 1, j == n_tiles - 1)
    p = p_ref[0].astype(F32)
    tile = p.shape[0]
    hb = V7X_BF16_SUBLANES
    prev = p_prev_ref[0].astype(F32)[hb - 1:hb, :] * jnp.where(starts, 0.0, 1.0)
    nxt = p_next_ref[0].astype(F32)[0:1, :] * jnp.where(ends, 0.0, 1.0)
    row = lax.broadcasted_iota(jnp.int32, p.shape, 0)
    p_dn = jnp.where(row == 0, prev, pltpu.roll(p, 1, 0))
    p_up = jnp.where(row == tile - 1, nxt, pltpu.roll(p, tile - 1, 0))
    w = w_ref[...]
    conv = w[0:1, :] * p_dn + w[1:2, :] * p + w[2:3, :] * p_up
    return (g_ref[0].astype(F32) * conv).astype(BF16)


def _outproj_kernel(*refs, d, has_next, conv, row_off, n_ctx_tiles, n_tiles):
    refs = list(refs)
    if conv:
        a1 = _conv_mix(*refs[:5], pl.program_id(1) + row_off, n_ctx_tiles, n_tiles)
        refs = refs[5:]
    else:
        a1 = refs[0][0]
        refs = refs[1:]
    if has_next:
        a2_ref, w_ref, h_ref, modc_ref, lng_ref, lnb_ref, modn_ref, hn_ref, u_ref = refs
    else:
        a2_ref, w_ref, h_ref, modc_ref, lng_ref, lnb_ref, hn_ref = refs
    y = _dot(a1, w_ref[0]) + _dot(a2_ref[0], w_ref[1])
    gate = modc_ref[0][:, 2 * d:3 * d]
    r = DEEPNORM_ALPHA * h_ref[0] + gate * y
    mu = jnp.mean(r, axis=-1, keepdims=True)
    xc = r - mu
    var = jnp.mean(xc * xc, axis=-1, keepdims=True)
    hn = xc * lax.rsqrt(var + LN_EPS) * lng_ref[...] + lnb_ref[...]
    hn_ref[0] = hn
    if has_next:
        modn = modn_ref[0]
        u_ref[0] = (hn * (1.0 + modn[:, d:2 * d]) + modn[:, 0:d]).astype(BF16)


def _outproj(a1_arr, a1_blk, a2_arr, a2_blk, w2, h, mod_cur, mod_next, ln_g, ln_b, n_ctx_tiles,
             conv_w=None):
    bsz, s_len, d = h.shape
    tile = ROW_TILE
    nt = s_len // tile
    has_next = mod_next is not None
    off = 0 if has_next else n_ctx_tiles
    mod_map = _mod_row_map(n_ctx_tiles - off, bsz)
    row_blk = lambda b, j: (b, j + off, 0)

    def part_spec(part):
        return pl.BlockSpec((1, tile, PART), lambda b, j: (b, j + off, part))

    if conv_w is not None:
        hb = V7X_BF16_SUBLANES
        per = tile // hb
        n_halo = s_len // hb
        in_specs = [
            part_spec(CONV_G_PART), part_spec(CONV_P_PART),
            pl.BlockSpec((1, hb, PART),
                         lambda b, j: (b, jnp.maximum((j + off) * per - 1, 0), CONV_P_PART)),
            pl.BlockSpec((1, hb, PART),
                         lambda b, j: (b, jnp.minimum((j + off + 1) * per, n_halo - 1), CONV_P_PART)),
            pl.BlockSpec(conv_w.shape, lambda b, j: (0, 0)),
        ]
        args = [a1_arr, a1_arr, a1_arr, a1_arr, conv_w]
    else:
        in_specs = [part_spec(a1_blk)]
        args = [a1_arr]
    in_specs += [
        part_spec(a2_blk),
        pl.BlockSpec(w2.shape, lambda b, j: (0, 0, 0)),
        pl.BlockSpec((1, tile, d), row_blk),
        pl.BlockSpec((1, 1, 3 * d), mod_map),
        pl.BlockSpec((1, d), lambda b, j: (0, 0)),
        pl.BlockSpec((1, d), lambda b, j: (0, 0)),
    ]
    args += [a2_arr, w2, h, mod_cur, ln_g, ln_b]
    out_specs = [pl.BlockSpec((1, tile, d), lambda b, j: (b, j, 0))]
    out_shape = [jax.ShapeDtypeStruct((bsz, s_len - off * tile, d), F32)]
    if has_next:
        in_specs.append(pl.BlockSpec((1, 1, 3 * d), mod_map))
        args.append(mod_next)
        out_specs.append(pl.BlockSpec((1, tile, d), lambda b, j: (b, j, 0)))
        out_shape.append(jax.ShapeDtypeStruct((bsz, s_len, d), BF16))
    kern = functools.partial(_outproj_kernel, d=d, has_next=has_next, conv=conv_w is not None,
                             row_off=off, n_ctx_tiles=n_ctx_tiles, n_tiles=nt)
    vm = (2 * (3 * tile * PART * 2 + 2 * PART * d * 2 + tile * d * (4 + 4 + 2))
          + 10 * tile * d * 4)
    res = pl.pallas_call(
        kern,
        grid=(bsz, nt - off),
        in_specs=in_specs,
        out_specs=out_specs,
        out_shape=out_shape,
        compiler_params=_params(2, vm),
        name="outproj_norm",
    )(*args)
    return res if has_next else res[0]


def _rope_pair(x, cs):
    prod = x * cs
    return prod + pltpu.roll(prod, QK_ROPE_DIM, 1)


def _mla_inproj_kernel(u_ref, win_ref, qn_ref, kvn_ref, wuq_ref, wukv_ref, cs_ref,
                       qp_ref, kp_ref, va_ref, z_ref):
    ql, kvl, lanes = Q_LORA_RANK, KV_LORA_RANK, V7X_LANES
    t = _dot(u_ref[0], win_ref[...])
    q_c = t[:, 0:ql]
    kv_c = t[:, ql:ql + kvl]
    kr2 = t[:, ql + kvl:ql + kvl + lanes]
    z_ref[0] = t[:, ql + kvl + lanes:].astype(BF16)
    cs = cs_ref[...]

    lane = lax.broadcasted_iota(jnp.int32, kr2.shape, 1)
    krt = jnp.where(lane < QK_ROPE_DIM, _rope_pair(kr2, cs), 0.0).astype(BF16)
    ones = jnp.ones(kr2.shape, BF16)

    qn = q_c * lax.rsqrt(jnp.mean(q_c * q_c, axis=-1, keepdims=True) + RMS_EPS) * qn_ref[...]
    q = _dot(qn.astype(BF16), wuq_ref[...])
    q_scale = MLA_SCALE * LOG2_E
    kvn = kv_c * lax.rsqrt(jnp.mean(kv_c * kv_c, axis=-1, keepdims=True) + RMS_EPS) * kvn_ref[...]
    kv = _dot(kvn.astype(BF16), wukv_ref[...])
    half = MLA_HEADS * QK_NOPE_DIM
    for h in range(MLA_HEADS):
        base = h * Q_HEAD_PAD
        qp_ref[0, :, base:base + lanes] = (q[:, base:base + lanes] * q_scale).astype(BF16)
        qp_ref[0, :, base + lanes:base + 2 * lanes] = (
            _rope_pair(q[:, base + lanes:base + 2 * lanes], cs) * q_scale).astype(BF16)
        kp_ref[0, :, base:base + lanes] = kv[:, h * lanes:(h + 1) * lanes].astype(BF16)
        kp_ref[0, :, base + lanes:base + 2 * lanes] = krt
        va_ref[0, :, base:base + lanes] = kv[:, half + h * lanes:half + (h + 1) * lanes].astype(BF16)
        va_ref[0, :, base + lanes:base + 2 * lanes] = ones


def _mla_inproj(u, w_in, q_norm, kv_norm, w_uq, w_ukv, cs):
    bsz, s_len, d = u.shape
    tile = ROW_TILE
    nt = s_len // tile
    n_in = w_in.shape[1]
    nq = w_uq.shape[1]
    nkv = w_ukv.shape[1]
    zdim = MLA_DIM
    full = lambda a: pl.BlockSpec(a.shape, lambda b, j: (0,) * a.ndim)
    row = lambda w: pl.BlockSpec((1, tile, w), lambda b, j: (b, j, 0))
    vm = (2 * (tile * d * 2 + d * n_in * 2 + w_uq.size * 2 + w_ukv.size * 2
               + tile * (3 * nq + zdim) * 2)
          + tile * (n_in + nq + nkv) * 4 * 2)
    return pl.pallas_call(
        _mla_inproj_kernel,
        grid=(bsz, nt),
        in_specs=[row(d), full(w_in), full(q_norm), full(kv_norm), full(w_uq), full(w_ukv),
                  pl.BlockSpec((tile, V7X_LANES), lambda b, j: (j, 0))],
        out_specs=[row(nq), row(nq), row(nq), row(zdim)],
        out_shape=[jax.ShapeDtypeStruct((bsz, s_len, nq), BF16),
                   jax.ShapeDtypeStruct((bsz, s_len, nq), BF16),
                   jax.ShapeDtypeStruct((bsz, s_len, nq), BF16),
                   jax.ShapeDtypeStruct((bsz, s_len, zdim), BF16)],
        compiler_params=_params(2, vm),
        name="mla_inproj",
    )(u, w_in, q_norm, kv_norm, w_uq, w_ukv, cs)


def _mla_attn_kernel(q_ref, kp_ref, va_ref, z_ref, o_ref,
                     *, q_off, n_ctx_tiles, c_len, s_len, heads):
    jq = pl.program_id(2)
    lanes = V7X_LANES

    def scores(g, n_keys):
        blk = slice(g * Q_HEAD_PAD, (g + 1) * Q_HEAD_PAD)
        return _dot_nt(q_ref[0, :, blk], kp_ref[0, 0:n_keys, blk])

    def softmax(s):
        return jnp.exp2(s - jnp.max(s, axis=1, keepdims=True)).astype(BF16)

    def values(g, p, n_keys):
        oa = _dot(p, va_ref[0, 0:n_keys, g * Q_HEAD_PAD:(g + 1) * Q_HEAD_PAD])
        o = oa[:, 0:lanes] / oa[:, lanes:2 * lanes]
        zg = z_ref[0, :, g * lanes:(g + 1) * lanes].astype(F32)
        o_ref[0, :, g * lanes:(g + 1) * lanes] = (o * _silu(zg)).astype(BF16)

    def attend(n_keys):
        s = {0: scores(0, n_keys)}
        p = {}
        for g in range(heads + 1):
            if g + 1 < heads:
                s[g + 1] = scores(g + 1, n_keys)
            if g < heads:
                p[g] = softmax(s.pop(g))
            if g >= 1:
                values(g - 1, p.pop(g - 1), n_keys)

    if q_off < n_ctx_tiles:
        @pl.when(jq + q_off < n_ctx_tiles)
        def _():
            attend(c_len)

        @pl.when(jq + q_off >= n_ctx_tiles)
        def _():
            attend(s_len)
    else:
        attend(s_len)


def _mla_attention(qp, kp, va, z, n_ctx_tiles, need_ctx):
    bsz, s_len, _ = qp.shape
    tile = ROW_TILE
    nt = s_len // tile
    q_off = 0 if need_ctx else n_ctx_tiles
    heads = ATTN_HEADS_PER_STEP
    kern = functools.partial(_mla_attn_kernel, q_off=q_off, n_ctx_tiles=n_ctx_tiles,
                             c_len=n_ctx_tiles * tile, s_len=s_len, heads=heads)
    lanes = V7X_LANES
    wide = heads * Q_HEAD_PAD
    vm = (2 * (tile * wide * 2 + 2 * s_len * wide * 2 + 2 * tile * heads * lanes * 2)
          + 3 * tile * s_len * (4 + 2))
    return pl.pallas_call(
        kern,
        grid=(bsz, MLA_HEADS // heads, nt - q_off),
        in_specs=[
            pl.BlockSpec((1, tile, wide), lambda b, h, j: (b, j + q_off, h)),
            pl.BlockSpec((1, s_len, wide), lambda b, h, j: (b, 0, h)),
            pl.BlockSpec((1, s_len, wide), lambda b, h, j: (b, 0, h)),
            pl.BlockSpec((1, tile, heads * lanes), lambda b, h, j: (b, j + q_off, h)),
        ],
        out_specs=pl.BlockSpec((1, tile, heads * lanes), lambda b, h, j: (b, j + q_off, h)),
        out_shape=jax.ShapeDtypeStruct((bsz, s_len, MLA_DIM), BF16),
        compiler_params=_params(3, vm),
        name="mla_attention",
    )(qp, kp, va, z)


def _even_weights(w_in, gate_b, w_out):
    d = w_in.shape[0]
    n_main = N_PARTS * PART
    ng = w_in.shape[1] - n_main
    w_main = w_in[:, :n_main].reshape(d, N_PARTS, PART).transpose(1, 0, 2).astype(BF16)
    w_main = w_main[jnp.array(EVEN_PART_ORDER)]
    w_gate = jnp.pad(w_in[:, n_main:], ((0, 0), (0, GATE_PAD - ng))).astype(BF16)
    gb = jnp.pad(gate_b, (0, GATE_PAD - ng)).reshape(1, GATE_PAD).astype(F32)
    w_out2 = w_out.reshape(2, PART, w_out.shape[1]).astype(BF16)
    return w_main, w_gate, gb, w_out2


def _swap_halves(w):
    half = w.shape[-1] // 2
    return jnp.concatenate([w[..., half:], w[..., :half]], axis=-1)


def _odd_weights(w_in, w_uq, w_ukv, w_out):
    ql, kvl, rope = Q_LORA_RANK, KV_LORA_RANK, QK_ROPE_DIM
    w_kr = w_in[:, ql + kvl:ql + kvl + rope]
    w_in_x = jnp.concatenate([w_in[:, :ql + kvl], w_kr, _swap_halves(w_kr),
                              w_in[:, ql + kvl + rope:]], axis=1).astype(BF16)
    wq = w_uq.reshape(ql, MLA_HEADS, QK_NOPE_DIM + rope)
    wq_r = wq[:, :, QK_NOPE_DIM:]
    w_uq_x = jnp.concatenate([wq[:, :, :QK_NOPE_DIM], wq_r, _swap_halves(wq_r)], axis=2)
    w_uq_x = w_uq_x.reshape(ql, MLA_HEADS * Q_HEAD_PAD).astype(BF16)
    wkv = w_ukv.reshape(kvl, MLA_HEADS, QK_NOPE_DIM + V_HEAD_DIM)
    w_ukv_x = jnp.concatenate([wkv[:, :, :QK_NOPE_DIM].reshape(kvl, -1),
                               wkv[:, :, QK_NOPE_DIM:].reshape(kvl, -1)], axis=1).astype(BF16)
    w_out2 = w_out.reshape(2, PART, w_out.shape[1]).astype(BF16)
    return w_in_x, w_uq_x, w_ukv_x, w_out2


def _rope_table(t_len, c_len):
    rows = t_len // GRID_W
    row = jnp.repeat(jnp.arange(rows, dtype=F32), GRID_W)
    col = jnp.tile(jnp.arange(GRID_W, dtype=F32), rows)
    n_freq = QK_ROPE_DIM // 4
    inv_freq = ROPE_BASE ** (-jnp.arange(n_freq, dtype=F32) / n_freq)
    ang = jnp.concatenate([row[:, None] * inv_freq, col[:, None] * inv_freq], axis=-1)
    cos, sin = jnp.cos(ang), jnp.sin(ang)
    lat = jnp.concatenate([cos, cos, -sin, sin], axis=-1)
    half = QK_ROPE_DIM
    ctx = jnp.concatenate([jnp.ones((c_len, half), F32), jnp.zeros((c_len, half), F32)], axis=-1)
    return jnp.concatenate([ctx, lat], axis=0)


def kernel(x, c, ctx, c_ctx, mod_w, mod_b, ln_g, ln_b, ev_w_in, ev_conv_w, ev_gate_b, ev_w_out,
           od_w_in, od_q_norm, od_w_uq, od_kv_norm, od_w_ukv, od_w_out):
    bsz, t_len, d = x.shape
    c_len = ctx.shape[1]
    s_len = c_len + t_len
    assert d == PART and c_len % ROW_TILE == 0 and t_len % ROW_TILE == 0
    assert (bsz * s_len) % INPROJ_ROWS == 0 and t_len % GRID_W == 0
    nct = c_len // ROW_TILE

    n_rows = -(-(bsz + 1) // MOD_ROWS_PAD) * MOD_ROWS_PAD
    cc = jnp.concatenate([c, c_ctx[None, :], jnp.zeros((n_rows - bsz - 1, d), F32)], axis=0)
    mods = _modulation(cc, mod_w, mod_b).reshape(DEPTH, n_rows, 1, 3 * d)

    cs = _rope_table(t_len, c_len)
    h, u = _prep(x, ctx, mods[0])

    for layer in range(DEPTH):
        j = layer // 2
        last = layer == DEPTH - 1
        mod_next = None if last else mods[layer + 1]
        lng, lnb = ln_g[layer].reshape(1, d), ln_b[layer].reshape(1, d)
        if layer % 2 == 0:
            w_main, w_gate, gb, w_out2 = _even_weights(ev_w_in[j], ev_gate_b[j], ev_w_out[j])
            main, gates = _even_inproj(u.reshape(bsz * s_len, d), w_main, w_gate, gb)
            main = main.reshape(bsz, s_len, (N_PARTS - N_PAIRED) * PART)
            gate_cols, gate_rows = _gate_prep(gates.reshape(bsz, s_len, GATE_PAD))
            mix_b = _mlstm(main, gate_cols, gate_rows, nct)
            res = _outproj(main, None, mix_b, 0, w_out2, h, mods[layer], mod_next, lng, lnb, nct,
                           conv_w=ev_conv_w[j])
        else:
            w_in_x, w_uq_x, w_ukv_x, w_out2 = _odd_weights(od_w_in[j], od_w_uq[j], od_w_ukv[j],
                                                           od_w_out[j])
            qp, kp, va, z = _mla_inproj(u, w_in_x, od_q_norm[j].reshape(1, -1),
                                        od_kv_norm[j].reshape(1, -1), w_uq_x, w_ukv_x, cs)
            og = _mla_attention(qp, kp, va, z, nct, need_ctx=not last)
            res = _outproj(og, 0, og, 1, w_out2, h, mods[layer], mod_next, lng, lnb, nct)
        if last:
            return res
        h, u = res
```

```python
import functools

import jax
import jax.numpy as jnp
from jax import lax
from jax.experimental import pallas as pl
from jax.experimental.pallas import tpu as pltpu

DEPTH = 4
GRID_W = 64
CONV_DIM = 1024
MLSTM_HEADS = 4
MLSTM_HEAD_DIM = 256
MLSTM_DIM = MLSTM_HEADS * MLSTM_HEAD_DIM
MLSTM_K_SCALE = MLSTM_HEAD_DIM ** -0.5
NEG_INIT = -1e30
MLA_HEADS = 16
QK_NOPE_DIM = 128
QK_ROPE_DIM = 64
V_HEAD_DIM = 128
Q_LORA_RANK = 384
KV_LORA_RANK = 256
MLA_DIM = MLA_HEADS * V_HEAD_DIM
MLA_SCALE = (QK_NOPE_DIM + QK_ROPE_DIM) ** -0.5
ROPE_BASE = 10000.0
LN_EPS = 1e-5
RMS_EPS = 1e-6
DEEPNORM_ALPHA = (2 * DEPTH) ** 0.25

V7X_LANES = 128
V7X_MXU_DIM = 256
V7X_BF16_SUBLANES = 16
V7X_VMEM_BYTES = 64 * 1024 * 1024

ROW_TILE = V7X_MXU_DIM
INPROJ_ROWS = 1024
PART = 1024
N_PARTS = 9
N_PAIRED = 2
CONV_G_PART, CONV_P_PART = 0, 1
MLSTM_Q_PART = 2
EVEN_PART_ORDER = (0, 3, 1, 2, 4, 5, 6, 7, 8)
GATE_PAD = V7X_LANES
Q_HEAD_PAD = 2 * V7X_LANES
MOD_ROWS_PAD = 8
MLSTM_HEADS_PER_STEP = 2
ATTN_HEADS_PER_STEP = 8
LOG2_E = 1.4426950408889634

F32 = jnp.float32
BF16 = jnp.bfloat16


def _vmem_limit(nbytes):
    return int(min(nbytes + 16 * 1024 * 1024, V7X_VMEM_BYTES - 4 * 1024 * 1024))


def _params(ngrid, nbytes):
    return pltpu.CompilerParams(
        dimension_semantics=("arbitrary",) * ngrid,
        vmem_limit_bytes=_vmem_limit(nbytes))


def _sigmoid(x):
    return 0.5 * jnp.tanh(0.5 * x) + 0.5


def _silu(x):
    return x * _sigmoid(x)


def _dot(a, b):
    return jnp.dot(a, b, preferred_element_type=F32)


def _dot_nt(a, b):
    return lax.dot_general(a, b, (((1,), (1,)), ((), ())), preferred_element_type=F32)


def _mod_kernel(cc_ref, w_ref, b_ref, o_ref):
    cc = cc_ref[...]
    o_ref[0] = _dot(_silu(cc), w_ref[0]) + b_ref[0]


def _modulation(cc, mod_w, mod_b):
    rows, d = cc.shape
    depth, _, n3 = mod_w.shape
    nblk = n3 // d
    return pl.pallas_call(
        _mod_kernel,
        grid=(depth, nblk),
        in_specs=[
            pl.BlockSpec((rows, d), lambda l, n: (0, 0)),
            pl.BlockSpec((1, d, d), lambda l, n: (l, 0, n)),
            pl.BlockSpec((1, 1, d), lambda l, n: (l, 0, n)),
        ],
        out_specs=pl.BlockSpec((1, rows, d), lambda l, n: (l, 0, n)),
        out_shape=jax.ShapeDtypeStruct((depth, rows, n3), F32),
        compiler_params=_params(2, 4 * (rows * d * 4 + d * d * 4)),
        name="modulation",
    )(cc, mod_w, mod_b.reshape(depth, 1, n3))


def _prep_kernel(ctx_ref, x_ref, mod_ref, h_ref, u_ref, *, n_ctx_tiles, d):
    j = pl.program_id(1)
    val = jnp.where(j < n_ctx_tiles, ctx_ref[0], x_ref[0])
    mod = mod_ref[0]
    sh, sc = mod[:, 0:d], mod[:, d:2 * d]
    h_ref[0] = val
    u_ref[0] = (val * (1.0 + sc) + sh).astype(BF16)


def _mod_row_map(n_ctx_tiles, ctx_row):
    return lambda b, j: (jnp.where(j < n_ctx_tiles, ctx_row, b), 0, 0)


def _prep(x, ctx, mod0):
    bsz, t_len, d = x.shape
    c_len = ctx.shape[1]
    s_len = c_len + t_len
    nct, nt = c_len // ROW_TILE, s_len // ROW_TILE
    kern = functools.partial(_prep_kernel, n_ctx_tiles=nct, d=d)
    tile = (1, ROW_TILE, d)
    return pl.pallas_call(
        kern,
        grid=(bsz, nt),
        in_specs=[
            pl.BlockSpec(tile, lambda b, j: (b, jnp.minimum(j, nct - 1), 0)),
            pl.BlockSpec(tile, lambda b, j: (b, jnp.maximum(j - nct, 0), 0)),
            pl.BlockSpec((1, 1, 3 * d), _mod_row_map(nct, bsz)),
        ],
        out_specs=[pl.BlockSpec(tile, lambda b, j: (b, j, 0)),
                   pl.BlockSpec(tile, lambda b, j: (b, j, 0))],
        out_shape=[jax.ShapeDtypeStruct((bsz, s_len, d), F32),
                   jax.ShapeDtypeStruct((bsz, s_len, d), BF16)],
        compiler_params=_params(2, 2 * ROW_TILE * d * (4 + 4 + 4 + 2)),
        name="prep",
    )(ctx, x, mod0)


def _even_inproj_kernel(u_ref, w_ref, wg_ref, gb_ref, main_ref, g_ref):
    n = pl.program_id(1)
    u = u_ref[...]

    @pl.when(n == 0)
    def _():
        main_ref[...] = (_dot(u, w_ref[0]) * _silu(_dot(u, w_ref[1]))).astype(BF16)
        g_ref[...] = _dot(u, wg_ref[...]) + gb_ref[...]

    @pl.when(n == 1)
    def _():
        main_ref[...] = (_dot(u, w_ref[2]) * _dot(u, w_ref[3])).astype(BF16)

    @pl.when(n >= N_PAIRED)
    def _():
        main_ref[...] = _dot(u, w_ref[n + N_PAIRED]).astype(BF16)


def _even_inproj(u2d, w_main, w_gate, gate_b):
    rows, d = u2d.shape
    tm = INPROJ_ROWS
    n_out = N_PARTS - N_PAIRED
    vm = (2 * tm * d * 2 + N_PARTS * d * PART * 2 + 2 * tm * PART * 2
          + 3 * tm * PART * 4 + 2 * tm * GATE_PAD * 4 + 2 * d * GATE_PAD * 2)
    return pl.pallas_call(
        _even_inproj_kernel,
        grid=(rows // tm, n_out),
        in_specs=[
            pl.BlockSpec((tm, d), lambda i, n: (i, 0)),
            pl.BlockSpec((N_PARTS, d, PART), lambda i, n: (0, 0, 0),
                         pipeline_mode=pl.Buffered(1)),
            pl.BlockSpec((d, GATE_PAD), lambda i, n: (0, 0)),
            pl.BlockSpec((1, GATE_PAD), lambda i, n: (0, 0)),
        ],
        out_specs=[pl.BlockSpec((tm, PART), lambda i, n: (i, n)),
                   pl.BlockSpec((tm, GATE_PAD), lambda i, n: (i, 0))],
        out_shape=[jax.ShapeDtypeStruct((rows, n_out * PART), BF16),
                   jax.ShapeDtypeStruct((rows, GATE_PAD), F32)],
        compiler_params=_params(2, vm),
        name="even_inproj",
    )(u2d, w_main, w_gate, gate_b)


def _log_sigmoid(x):
    return jnp.minimum(x, 0.0) - jnp.log1p(jnp.exp(-jnp.abs(x)))


def _lane_scan(x, op, ident, reverse):
    n = x.shape[1]
    lane = lax.broadcasted_iota(jnp.int32, x.shape, 1)
    k = 1
    while k < n:
        if reverse:
            shifted = jnp.where(lane < n - k, pltpu.roll(x, n - k, 1), ident)
        else:
            shifted = jnp.where(lane >= k, pltpu.roll(x, k, 1), ident)
        x = op(x, shifted)
        k *= 2
    return x


def _gate_prep_kernel(g_ref, c_ref, r_ref, *, n_chunks, chunk):
    nh = MLSTM_HEADS
    assert 2 * nh == 8
    fwd, bwd = [], []
    for c in range(n_chunks):
        gt = g_ref[0, c * chunk:(c + 1) * chunk, :].T
        fwd.append(gt[0:8])
        bwd.append(gt[8:16])

    def scans(g, reverse):
        n_rows = g.shape[0]
        logf = pltpu.roll(_log_sigmoid(g) * LOG2_E, n_rows - nh, 0)
        b = _lane_scan(logf, jnp.add, 0.0, reverse)
        a = g * LOG2_E - b
        return b, a, _lane_scan(a, jnp.maximum, -jnp.inf, reverse)

    kinds = scans(jnp.concatenate(fwd, axis=0), False) + scans(jnp.concatenate(bwd, axis=0), True)
    sub = lax.broadcasted_iota(jnp.int32, (8, chunk), 0)
    pad = jnp.zeros((V7X_LANES - 8, chunk), F32)
    for c in range(n_chunks):
        for h in range(nh):
            row = c * 8 + h
            x8 = jnp.zeros((8, chunk), F32)
            for idx, arr in enumerate(kinds):
                x8 = jnp.where(sub == idx, arr[row:row + 1, :], x8)
            r_ref[0, h, c] = x8
            c_ref[0, h, c * chunk:(c + 1) * chunk, :] = jnp.concatenate([x8, pad], axis=0).T


def _gate_prep(gates):
    bsz, s_len, gp = gates.shape
    chunk = ROW_TILE
    nch = s_len // chunk
    nh = MLSTM_HEADS
    kern = functools.partial(_gate_prep_kernel, n_chunks=nch, chunk=chunk)
    vm = 2 * s_len * gp * 4 * (1 + nh) + 2 * nh * nch * 8 * chunk * 4
    return pl.pallas_call(
        kern,
        grid=(bsz,),
        in_specs=[pl.BlockSpec((1, s_len, gp), lambda b: (b, 0, 0))],
        out_specs=[pl.BlockSpec((1, nh, s_len, V7X_LANES), lambda b: (b, 0, 0, 0)),
                   pl.BlockSpec((1, nh, nch, 8, chunk), lambda b: (b, 0, 0, 0, 0))],
        out_shape=[jax.ShapeDtypeStruct((bsz, nh, s_len, V7X_LANES), F32),
                   jax.ShapeDtypeStruct((bsz, nh, nch, 8, chunk), F32)],
        compiler_params=_params(1, vm),
        name="gate_prep",
    )(gates)


def _mlstm_chunk(q, k, v, cols, a_row, m_state, c_prev, n_prev, direction, mask):
    chunk = q.shape[0]
    lanes = V7X_LANES
    last = chunk - 1 if direction == 0 else 0
    col0 = 3 * direction
    wide = lambda col: jnp.broadcast_to(col, (chunk, lanes))
    twice = lambda x: jnp.concatenate([x, x], axis=1)
    b_w = wide(cols[:, col0:col0 + 1])
    a_w = wide(cols[:, col0 + 1:col0 + 2])
    m_w = jnp.maximum(m_state, wide(cols[:, col0 + 2:col0 + 3]))
    ones = jnp.ones((chunk, lanes), BF16)

    w = jnp.exp2(jnp.where(mask, a_row - twice(m_w), -jnp.inf))
    a_inter = jnp.exp2(m_state - m_w)
    s = (_dot_nt(q, k) * w).astype(BF16)
    num = _dot(s, v) + twice(a_inter) * _dot(q, c_prev.astype(BF16))
    den = _dot(s, ones) + a_inter * _dot(q, n_prev.astype(BF16))
    h = num * twice(1.0 / jnp.maximum(jnp.abs(den), jnp.exp2(-(b_w + m_w))))

    m_last = m_w[last:last + 1, 0:1]
    decay = jnp.exp2(m_state - m_last)
    w_tok = jnp.exp2(a_w - m_last)
    k_t = k.astype(F32).T.astype(BF16)
    c_new = decay * c_prev + _dot(k_t, (v.astype(F32) * twice(w_tok)).astype(BF16))
    n_new = decay * n_prev + _dot(k_t, w_tok.astype(BF16))
    return h, c_new, n_new, b_w[last:last + 1, 0:1] + m_last


def _mlstm_kernel(q_ref, k_ref, v_ref, o_ref, z_ref, c_ref, r_ref, out_ref,
                  hf_ref, hb_ref, cst_ref, nst_ref, *, n_chunks, n_ctx_chunks, chunk, heads):
    dh = MLSTM_HEAD_DIM
    t_idx = lax.broadcasted_iota(jnp.int32, (chunk, chunk), 0)
    s_idx = lax.broadcasted_iota(jnp.int32, (chunk, chunk), 1)
    masks = (s_idx <= t_idx, s_idx >= t_idx)
    h_refs = (hf_ref, hb_ref)
    cst_ref[...] = jnp.zeros_like(cst_ref)
    nst_ref[...] = jnp.zeros_like(nst_ref)

    def body(jj, m_states):
        ci_bwd = jnp.where(jj < n_ctx_chunks, n_ctx_chunks - 1 - jj,
                           n_chunks - 1 - (jj - n_ctx_chunks))
        new_states = []
        for g in range(heads):
            lanes = slice(g * dh, (g + 1) * dh)
            for direction in range(2):
                ci = jj if direction == 0 else ci_bwd
                rows = pl.ds(pl.multiple_of(ci * chunk, chunk), chunk)
                idx = 2 * g + direction
                q = q_ref[0, rows, lanes]
                k = k_ref[0, rows, lanes] * jnp.asarray(MLSTM_K_SCALE, BF16)
                v = v_ref[0, rows, lanes]
                a_row = r_ref[0, g, ci][3 * direction + 1:3 * direction + 2, :]
                h, c_new, n_new, m_new = _mlstm_chunk(
                    q, k, v, c_ref[0, g, rows, :], a_row, m_states[idx],
                    cst_ref[idx], nst_ref[idx], direction, masks[direction])
                cst_ref[idx] = c_new
                nst_ref[idx] = n_new
                h_refs[direction][rows, lanes] = h
                new_states.append(m_new)
        return tuple(new_states)

    init = tuple(jnp.full((1, 1), NEG_INIT, F32) for _ in range(2 * heads))
    lax.fori_loop(0, n_chunks, body, init)

    def gate_body(ci, carry):
        rows = pl.ds(pl.multiple_of(ci * chunk, chunk), chunk)
        og = o_ref[0, rows, :].astype(F32)
        zg = z_ref[0, rows, :].astype(F32)
        out_ref[0, rows, :] = (_sigmoid(og) * (hf_ref[rows, :] + hb_ref[rows, :])
                               * _silu(zg)).astype(BF16)
        return carry

    lax.fori_loop(0, n_chunks, gate_body, 0)


def _mlstm(main, gate_cols, gate_rows, n_ctx_chunks):
    bsz, s_len, _ = main.shape
    nh, dh = MLSTM_HEADS, MLSTM_HEAD_DIM
    heads = MLSTM_HEADS_PER_STEP
    chunk = ROW_TILE
    assert chunk == dh
    nch = s_len // chunk
    width = heads * dh
    blocks_per_part = PART // width

    def part_spec(part):
        return pl.BlockSpec((1, s_len, width), lambda b, h: (b, 0, part * blocks_per_part + h))

    kern = functools.partial(_mlstm_kernel, n_chunks=nch, n_ctx_chunks=n_ctx_chunks,
                             chunk=chunk, heads=heads)
    vm = (2 * 6 * s_len * width * 2 + 2 * heads * s_len * V7X_LANES * 4
          + 2 * heads * nch * 8 * chunk * 4 + 2 * s_len * width * 4
          + 2 * heads * (dh + 1) * dh * 4 + 16 * chunk * chunk * 4)
    return pl.pallas_call(
        kern,
        grid=(bsz, nh // heads),
        in_specs=[part_spec(MLSTM_Q_PART + i) for i in range(5)] + [
                  pl.BlockSpec((1, heads, s_len, V7X_LANES), lambda b, h: (b, h, 0, 0)),
                  pl.BlockSpec((1, heads, nch, 8, chunk), lambda b, h: (b, h, 0, 0, 0))],
        out_specs=pl.BlockSpec((1, s_len, width), lambda b, h: (b, 0, h)),
        out_shape=jax.ShapeDtypeStruct((bsz, s_len, nh * dh), BF16),
        scratch_shapes=[pltpu.VMEM((s_len, width), F32),
                        pltpu.VMEM((s_len, width), F32),
                        pltpu.VMEM((2 * heads, dh, dh), F32),
                        pltpu.VMEM((2 * heads, dh, V7X_LANES), F32)],
        compiler_params=_params(2, vm),
        name="mlstm",
    )(main, main, main, main, main, gate_cols, gate_rows)


def _conv_mix(g_ref, p_ref, p_prev_ref, p_next_ref, w_ref, j, n_ctx_tiles, n_tiles):
    starts = jnp.logical_or(j == 0, j == n_ctx_tiles)
    ends = jnp.logical_or(j == n_ctx_tiles - 1, j == n_tiles - 1)
    p = p_ref[0].astype(F32)
    tile = p.shape[0]
    hb = V7X_BF16_SUBLANES
    prev = p_prev_ref[0].astype(F32)[hb - 1:hb, :] * jnp.where(starts, 0.0, 1.0)
    nxt = p_next_ref[0].astype(F32)[0:1, :] * jnp.where(ends, 0.0, 1.0)
    row = lax.broadcasted_iota(jnp.int32, p.shape, 0)
    p_dn = jnp.where(row == 0, prev, pltpu.roll(p, 1, 0))
    p_up = jnp.where(row == tile - 1, nxt, pltpu.roll(p, tile - 1, 0))
    w = w_ref[...]
    conv = w[0:1, :] * p_dn + w[1:2, :] * p + w[2:3, :] * p_up
    return (g_ref[0].astype(F32) * conv).astype(BF16)


def _outproj_kernel(*refs, d, has_next, conv, row_off, n_ctx_tiles, n_tiles):
    refs = list(refs)
    if conv:
        a1 = _conv_mix(*refs[:5], pl.program_id(1) + row_off, n_ctx_tiles, n_tiles)
        refs = refs[5:]
    else:
        a1 = refs[0][0]
        refs = refs[1:]
    if has_next:
        a2_ref, w_ref, h_ref, modc_ref, lng_ref, lnb_ref, modn_ref, hn_ref, u_ref = refs
    else:
        a2_ref, w_ref, h_ref, modc_ref, lng_ref, lnb_ref, hn_ref = refs
    y = _dot(a1, w_ref[0]) + _dot(a2_ref[0], w_ref[1])
    gate = modc_ref[0][:, 2 * d:3 * d]
    r = DEEPNORM_ALPHA * h_ref[0] + gate * y
    mu = jnp.mean(r, axis=-1, keepdims=True)
    xc = r - mu
    var = jnp.mean(xc * xc, axis=-1, keepdims=True)
    hn = xc * lax.rsqrt(var + LN_EPS) * lng_ref[...] + lnb_ref[...]
    hn_ref[0] = hn
    if has_next:
        modn = modn_ref[0]
        u_ref[0] = (hn * (1.0 + modn[:, d:2 * d]) + modn[:, 0:d]).astype(BF16)


def _outproj(a1_arr, a1_blk, a2_arr, a2_blk, w2, h, mod_cur, mod_next, ln_g, ln_b, n_ctx_tiles,
             conv_w=None, a_row_shift=0):
    bsz, s_len, d = h.shape
    tile = ROW_TILE
    nt = s_len // tile
    has_next = mod_next is not None
    off = 0 if has_next else n_ctx_tiles
    mod_map = _mod_row_map(n_ctx_tiles - off, bsz)
    row_blk = lambda b, j: (b, j + off, 0)

    def part_spec(part, shift=0):
        return pl.BlockSpec((1, tile, PART), lambda b, j: (b, j + off + shift, part))

    if conv_w is not None:
        assert a_row_shift == 0
        hb = V7X_BF16_SUBLANES
        per = tile // hb
        n_halo = s_len // hb
        in_specs = [
            part_spec(CONV_G_PART), part_spec(CONV_P_PART),
            pl.BlockSpec((1, hb, PART),
                         lambda b, j: (b, jnp.maximum((j + off) * per - 1, 0), CONV_P_PART)),
            pl.BlockSpec((1, hb, PART),
                         lambda b, j: (b, jnp.minimum((j + off + 1) * per, n_halo - 1), CONV_P_PART)),
            pl.BlockSpec(conv_w.shape, lambda b, j: (0, 0)),
        ]
        args = [a1_arr, a1_arr, a1_arr, a1_arr, conv_w]
    else:
        in_specs = [part_spec(a1_blk, a_row_shift)]
        args = [a1_arr]
    in_specs += [
        part_spec(a2_blk, a_row_shift),
        pl.BlockSpec(w2.shape, lambda b, j: (0, 0, 0)),
        pl.BlockSpec((1, tile, d), row_blk),
        pl.BlockSpec((1, 1, 3 * d), mod_map),
        pl.BlockSpec((1, d), lambda b, j: (0, 0)),
        pl.BlockSpec((1, d), lambda b, j: (0, 0)),
    ]
    args += [a2_arr, w2, h, mod_cur, ln_g, ln_b]
    out_specs = [pl.BlockSpec((1, tile, d), lambda b, j: (b, j, 0))]
    out_shape = [jax.ShapeDtypeStruct((bsz, s_len - off * tile, d), F32)]
    if has_next:
        in_specs.append(pl.BlockSpec((1, 1, 3 * d), mod_map))
        args.append(mod_next)
        out_specs.append(pl.BlockSpec((1, tile, d), lambda b, j: (b, j, 0)))
        out_shape.append(jax.ShapeDtypeStruct((bsz, s_len, d), BF16))
    kern = functools.partial(_outproj_kernel, d=d, has_next=has_next, conv=conv_w is not None,
                             row_off=off, n_ctx_tiles=n_ctx_tiles, n_tiles=nt)
    vm = (2 * (3 * tile * PART * 2 + 2 * PART * d * 2 + tile * d * (4 + 4 + 2))
          + 10 * tile * d * 4)
    res = pl.pallas_call(
        kern,
        grid=(bsz, nt - off),
        in_specs=in_specs,
        out_specs=out_specs,
        out_shape=out_shape,
        compiler_params=_params(2, vm),
        name="outproj_norm",
    )(*args)
    return res if has_next else res[0]


def _rope_pair(x, cs):
    prod = x * cs
    return prod + pltpu.roll(prod, QK_ROPE_DIM, 1)


def _mla_inproj_kernel(u_ref, win_ref, qn_ref, kvn_ref, wuq_ref, wukv_ref, cs_ref,
                       qp_ref, kp_ref, va_ref, z_ref):
    ql, kvl, lanes = Q_LORA_RANK, KV_LORA_RANK, V7X_LANES
    t = _dot(u_ref[0], win_ref[...])
    q_c = t[:, 0:ql]
    kv_c = t[:, ql:ql + kvl]
    kr2 = t[:, ql + kvl:ql + kvl + lanes]
    z_ref[0] = t[:, ql + kvl + lanes:].astype(BF16)
    cs = cs_ref[...]

    lane = lax.broadcasted_iota(jnp.int32, kr2.shape, 1)
    krt = jnp.where(lane < QK_ROPE_DIM, _rope_pair(kr2, cs), 0.0).astype(BF16)
    ones = jnp.ones(kr2.shape, BF16)

    qn = q_c * lax.rsqrt(jnp.mean(q_c * q_c, axis=-1, keepdims=True) + RMS_EPS) * qn_ref[...]
    q = _dot(qn.astype(BF16), wuq_ref[...])
    q_scale = MLA_SCALE * LOG2_E
    kvn = kv_c * lax.rsqrt(jnp.mean(kv_c * kv_c, axis=-1, keepdims=True) + RMS_EPS) * kvn_ref[...]
    kv = _dot(kvn.astype(BF16), wukv_ref[...])
    half = MLA_HEADS * QK_NOPE_DIM
    for h in range(MLA_HEADS):
        base = h * Q_HEAD_PAD
        qp_ref[0, :, base:base + lanes] = (q[:, base:base + lanes] * q_scale).astype(BF16)
        qp_ref[0, :, base + lanes:base + 2 * lanes] = (
            _rope_pair(q[:, base + lanes:base + 2 * lanes], cs) * q_scale).astype(BF16)
        kp_ref[0, :, base:base + lanes] = kv[:, h * lanes:(h + 1) * lanes].astype(BF16)
        kp_ref[0, :, base + lanes:base + 2 * lanes] = krt
        va_ref[0, :, base:base + lanes] = kv[:, half + h * lanes:half + (h + 1) * lanes].astype(BF16)
        va_ref[0, :, base + lanes:base + 2 * lanes] = ones


def _mla_inproj(u, w_in, q_norm, kv_norm, w_uq, w_ukv, cs):
    bsz, s_len, d = u.shape
    tile = ROW_TILE
    nt = s_len // tile
    n_in = w_in.shape[1]
    nq = w_uq.shape[1]
    nkv = w_ukv.shape[1]
    zdim = MLA_DIM
    full = lambda a: pl.BlockSpec(a.shape, lambda b, j: (0,) * a.ndim)
    row = lambda w: pl.BlockSpec((1, tile, w), lambda b, j: (b, j, 0))
    vm = (2 * (tile * d * 2 + d * n_in * 2 + w_uq.size * 2 + w_ukv.size * 2
               + tile * (3 * nq + zdim) * 2)
          + tile * (n_in + nq + nkv) * 4 * 2)
    return pl.pallas_call(
        _mla_inproj_kernel,
        grid=(bsz, nt),
        in_specs=[row(d), full(w_in), full(q_norm), full(kv_norm), full(w_uq), full(w_ukv),
                  pl.BlockSpec((tile, V7X_LANES), lambda b, j: (j, 0))],
        out_specs=[row(nq), row(nq), row(nq), row(zdim)],
        out_shape=[jax.ShapeDtypeStruct((bsz, s_len, nq), BF16),
                   jax.ShapeDtypeStruct((bsz, s_len, nq), BF16),
                   jax.ShapeDtypeStruct((bsz, s_len, nq), BF16),
                   jax.ShapeDtypeStruct((bsz, s_len, zdim), BF16)],
        compiler_params=_params(2, vm),
        name="mla_inproj",
    )(u, w_in, q_norm, kv_norm, w_uq, w_ukv, cs)


def _mla_attn_kernel(qa_ref, qb_ref, kp_ref, va_ref, za_ref, zb_ref, o_ref,
                     *, step_off, pad_tiles, n_ctx_steps, c_len, s_len, heads, tile):
    jg = pl.program_id(2) + step_off
    lanes = V7X_LANES
    q_refs, z_refs = (qa_ref, qb_ref), (za_ref, zb_ref)

    def scores(unit, n_keys):
        t, g = unit
        blk = slice(g * Q_HEAD_PAD, (g + 1) * Q_HEAD_PAD)
        return _dot_nt(q_refs[t][0, :, blk], kp_ref[0, 0:n_keys, blk])

    def softmax(s):
        return jnp.exp2(s - jnp.max(s, axis=1, keepdims=True)).astype(BF16)

    def values(unit, p, n_keys):
        t, g = unit
        oa = _dot(p, va_ref[0, 0:n_keys, g * Q_HEAD_PAD:(g + 1) * Q_HEAD_PAD])
        o = oa[:, 0:lanes] / oa[:, lanes:2 * lanes]
        zg = z_refs[t][0, :, g * lanes:(g + 1) * lanes].astype(F32)
        o_ref[0, t * tile:(t + 1) * tile, g * lanes:(g + 1) * lanes] = (o * _silu(zg)).astype(BF16)

    def attend(tiles, n_keys):
        units = [(t, g) for t in tiles for g in range(heads)]
        n = len(units)
        s = {0: scores(units[0], n_keys)}
        p = {}
        for i in range(n + 1):
            if i + 1 < n:
                s[i + 1] = scores(units[i + 1], n_keys)
            if i < n:
                p[i] = softmax(s.pop(i))
            if i >= 1:
                values(units[i - 1], p.pop(i - 1), n_keys)

    if step_off < n_ctx_steps:
        if pad_tiles:
            @pl.when(jg == 0)
            def _():
                attend((1,), c_len)
        if n_ctx_steps > pad_tiles:
            @pl.when(jnp.logical_and(jg >= pad_tiles, jg < n_ctx_steps))
            def _():
                attend((0, 1), c_len)

        @pl.when(jg >= n_ctx_steps)
        def _():
            attend((0, 1), s_len)
    else:
        attend((0, 1), s_len)


def _attn_pad_tiles(n_ctx_tiles):
    return n_ctx_tiles % 2


def _mla_attention(qp, kp, va, z, n_ctx_tiles, need_ctx):
    bsz, s_len, _ = qp.shape
    tile = ROW_TILE
    nt = s_len // tile
    pad = _attn_pad_tiles(n_ctx_tiles)
    assert (nt - n_ctx_tiles) % 2 == 0
    n_steps = (pad + nt) // 2
    n_ctx_steps = (pad + n_ctx_tiles) // 2
    step_off = 0 if need_ctx else n_ctx_steps
    heads = ATTN_HEADS_PER_STEP
    kern = functools.partial(_mla_attn_kernel, step_off=step_off, pad_tiles=pad,
                             n_ctx_steps=n_ctx_steps, c_len=n_ctx_tiles * tile, s_len=s_len,
                             heads=heads, tile=tile)
    lanes = V7X_LANES
    wide = heads * Q_HEAD_PAD
    tile_a = lambda j: jnp.maximum(2 * (j + step_off) - pad, 0)
    tile_b = lambda j: 2 * (j + step_off) + 1 - pad
    vm = (2 * (2 * tile * wide * 2 + 2 * s_len * wide * 2 + 4 * tile * heads * lanes * 2)
          + 3 * tile * s_len * (4 + 2))
    return pl.pallas_call(
        kern,
        grid=(bsz, MLA_HEADS // heads, n_steps - step_off),
        in_specs=[
            pl.BlockSpec((1, tile, wide), lambda b, h, j: (b, tile_a(j), h)),
            pl.BlockSpec((1, tile, wide), lambda b, h, j: (b, tile_b(j), h)),
            pl.BlockSpec((1, s_len, wide), lambda b, h, j: (b, 0, h)),
            pl.BlockSpec((1, s_len, wide), lambda b, h, j: (b, 0, h)),
            pl.BlockSpec((1, tile, heads * lanes), lambda b, h, j: (b, tile_a(j), h)),
            pl.BlockSpec((1, tile, heads * lanes), lambda b, h, j: (b, tile_b(j), h)),
        ],
        out_specs=pl.BlockSpec((1, 2 * tile, heads * lanes), lambda b, h, j: (b, j + step_off, h)),
        out_shape=jax.ShapeDtypeStruct((bsz, (pad + nt) * tile, MLA_DIM), BF16),
        compiler_params=_params(3, vm),
        name="mla_attention",
    )(qp, qp, kp, va, z, z)


def _even_weights(w_in, gate_b, w_out):
    d = w_in.shape[0]
    n_main = N_PARTS * PART
    ng = w_in.shape[1] - n_main
    w_main = w_in[:, :n_main].reshape(d, N_PARTS, PART).transpose(1, 0, 2).astype(BF16)
    w_main = w_main[jnp.array(EVEN_PART_ORDER)]
    w_gate = jnp.pad(w_in[:, n_main:], ((0, 0), (0, GATE_PAD - ng))).astype(BF16)
    gb = jnp.pad(gate_b, (0, GATE_PAD - ng)).reshape(1, GATE_PAD).astype(F32)
    w_out2 = w_out.reshape(2, PART, w_out.shape[1]).astype(BF16)
    return w_main, w_gate, gb, w_out2


def _swap_halves(w):
    half = w.shape[-1] // 2
    return jnp.concatenate([w[..., half:], w[..., :half]], axis=-1)


def _odd_weights(w_in, w_uq, w_ukv, w_out):
    ql, kvl, rope = Q_LORA_RANK, KV_LORA_RANK, QK_ROPE_DIM
    w_kr = w_in[:, ql + kvl:ql + kvl + rope]
    w_in_x = jnp.concatenate([w_in[:, :ql + kvl], w_kr, _swap_halves(w_kr),
                              w_in[:, ql + kvl + rope:]], axis=1).astype(BF16)
    wq = w_uq.reshape(ql, MLA_HEADS, QK_NOPE_DIM + rope)
    wq_r = wq[:, :, QK_NOPE_DIM:]
    w_uq_x = jnp.concatenate([wq[:, :, :QK_NOPE_DIM], wq_r, _swap_halves(wq_r)], axis=2)
    w_uq_x = w_uq_x.reshape(ql, MLA_HEADS * Q_HEAD_PAD).astype(BF16)
    wkv = w_ukv.reshape(kvl, MLA_HEADS, QK_NOPE_DIM + V_HEAD_DIM)
    w_ukv_x = jnp.concatenate([wkv[:, :, :QK_NOPE_DIM].reshape(kvl, -1),
                               wkv[:, :, QK_NOPE_DIM:].reshape(kvl, -1)], axis=1).astype(BF16)
    w_out2 = w_out.reshape(2, PART, w_out.shape[1]).astype(BF16)
    return w_in_x, w_uq_x, w_ukv_x, w_out2


def _rope_table(t_len, c_len):
    rows = t_len // GRID_W
    row = jnp.repeat(jnp.arange(rows, dtype=F32), GRID_W)
    col = jnp.tile(jnp.arange(GRID_W, dtype=F32), rows)
    n_freq = QK_ROPE_DIM // 4
    inv_freq = ROPE_BASE ** (-jnp.arange(n_freq, dtype=F32) / n_freq)
    ang = jnp.concatenate([row[:, None] * inv_freq, col[:, None] * inv_freq], axis=-1)
    cos, sin = jnp.cos(ang), jnp.sin(ang)
    lat = jnp.concatenate([cos, cos, -sin, sin], axis=-1)
    half = QK_ROPE_DIM
    ctx = jnp.concatenate([jnp.ones((c_len, half), F32), jnp.zeros((c_len, half), F32)], axis=-1)
    return jnp.concatenate([ctx, lat], axis=0)


def kernel(x, c, ctx, c_ctx, mod_w, mod_b, ln_g, ln_b, ev_w_in, ev_conv_w, ev_gate_b, ev_w_out,
           od_w_in, od_q_norm, od_w_uq, od_kv_norm, od_w_ukv, od_w_out):
    bsz, t_len, d = x.shape
    c_len = ctx.shape[1]
    s_len = c_len + t_len
    assert d == PART and c_len % ROW_TILE == 0 and t_len % ROW_TILE == 0
    assert (bsz * s_len) % INPROJ_ROWS == 0 and t_len % GRID_W == 0
    nct = c_len // ROW_TILE

    n_rows = -(-(bsz + 1) // MOD_ROWS_PAD) * MOD_ROWS_PAD
    cc = jnp.concatenate([c, c_ctx[None, :], jnp.zeros((n_rows - bsz - 1, d), F32)], axis=0)
    mods = _modulation(cc, mod_w, mod_b).reshape(DEPTH, n_rows, 1, 3 * d)

    cs = _rope_table(t_len, c_len)
    h, u = _prep(x, ctx, mods[0])

    for layer in range(DEPTH):
        j = layer // 2
        last = layer == DEPTH - 1
        mod_next = None if last else mods[layer + 1]
        lng, lnb = ln_g[layer].reshape(1, d), ln_b[layer].reshape(1, d)
        if layer % 2 == 0:
            w_main, w_gate, gb, w_out2 = _even_weights(ev_w_in[j], ev_gate_b[j], ev_w_out[j])
            main, gates = _even_inproj(u.reshape(bsz * s_len, d), w_main, w_gate, gb)
            main = main.reshape(bsz, s_len, (N_PARTS - N_PAIRED) * PART)
            gate_cols, gate_rows = _gate_prep(gates.reshape(bsz, s_len, GATE_PAD))
            mix_b = _mlstm(main, gate_cols, gate_rows, nct)
            res = _outproj(main, None, mix_b, 0, w_out2, h, mods[layer], mod_next, lng, lnb, nct,
                           conv_w=ev_conv_w[j])
        else:
            w_in_x, w_uq_x, w_ukv_x, w_out2 = _odd_weights(od_w_in[j], od_w_uq[j], od_w_ukv[j],
                                                           od_w_out[j])
            qp, kp, va, z = _mla_inproj(u, w_in_x, od_q_norm[j].reshape(1, -1),
                                        od_kv_norm[j].reshape(1, -1), w_uq_x, w_ukv_x, cs)
            og = _mla_attention(qp, kp, va, z, nct, need_ctx=not last)
            res = _outproj(og, 0, og, 1, w_out2, h, mods[layer], mod_next, lng, lnb, nct,
                           a_row_shift=_attn_pad_tiles(nct))
        if last:
            return res
        h, u = res
```

```python
import functools

import jax
import jax.numpy as jnp
from jax import lax
from jax.experimental import pallas as pl
from jax.experimental.pallas import tpu as pltpu

DEPTH = 4
GRID_W = 64
CONV_DIM = 1024
MLSTM_HEADS = 4
MLSTM_HEAD_DIM = 256
MLSTM_DIM = MLSTM_HEADS * MLSTM_HEAD_DIM
MLSTM_K_SCALE = MLSTM_HEAD_DIM ** -0.5
NEG_INIT = -1e30
MLA_HEADS = 16
QK_NOPE_DIM = 128
QK_ROPE_DIM = 64
V_HEAD_DIM = 128
Q_LORA_RANK = 384
KV_LORA_RANK = 256
MLA_DIM = MLA_HEADS * V_HEAD_DIM
MLA_SCALE = (QK_NOPE_DIM + QK_ROPE_DIM) ** -0.5
ROPE_BASE = 10000.0
LN_EPS = 1e-5
RMS_EPS = 1e-6
DEEPNORM_ALPHA = (2 * DEPTH) ** 0.25

V7X_LANES = 128
V7X_MXU_DIM = 256
V7X_BF16_SUBLANES = 16
V7X_VMEM_BYTES = 64 * 1024 * 1024

ROW_TILE = V7X_MXU_DIM
INPROJ_ROWS = 1024
PART = 1024
N_PARTS = 9
N_PAIRED = 2
CONV_G_PART, CONV_P_PART = 0, 1
MLSTM_Q_PART = 2
EVEN_PART_ORDER = (0, 3, 1, 2, 4, 5, 6, 7, 8)
GATE_PAD = V7X_LANES
Q_HEAD_PAD = 2 * V7X_LANES
MOD_ROWS_PAD = 8
PREP_BATCH = 4
MLSTM_HEADS_PER_STEP = 2
ATTN_HEADS_PER_STEP = 8
LOG2_E = 1.4426950408889634

F32 = jnp.float32
BF16 = jnp.bfloat16


def _vmem_limit(nbytes):
    return int(min(nbytes + 16 * 1024 * 1024, V7X_VMEM_BYTES - 4 * 1024 * 1024))


def _params(ngrid, nbytes):
    return pltpu.CompilerParams(
        dimension_semantics=("arbitrary",) * ngrid,
        vmem_limit_bytes=_vmem_limit(nbytes))


def _sigmoid(x):
    return 0.5 * jnp.tanh(0.5 * x) + 0.5


def _silu(x):
    return x * _sigmoid(x)


def _dot(a, b):
    return jnp.dot(a, b, preferred_element_type=F32)


def _dot_nt(a, b):
    return lax.dot_general(a, b, (((1,), (1,)), ((), ())), preferred_element_type=F32)


def _mod_kernel(cc_ref, w_ref, b_ref, o_ref):
    cc = cc_ref[...]
    o_ref[0] = _dot(_silu(cc), w_ref[0]) + b_ref[0]


def _modulation(cc, mod_w, mod_b):
    rows, d = cc.shape
    depth, _, n3 = mod_w.shape
    nblk = n3 // d
    return pl.pallas_call(
        _mod_kernel,
        grid=(depth, nblk),
        in_specs=[
            pl.BlockSpec((rows, d), lambda l, n: (0, 0)),
            pl.BlockSpec((1, d, d), lambda l, n: (l, 0, n)),
            pl.BlockSpec((1, 1, d), lambda l, n: (l, 0, n)),
        ],
        out_specs=pl.BlockSpec((1, rows, d), lambda l, n: (l, 0, n)),
        out_shape=jax.ShapeDtypeStruct((depth, rows, n3), F32),
        compiler_params=_params(2, 4 * (rows * d * 4 + d * d * 4)),
        name="modulation",
    )(cc, mod_w, mod_b.reshape(depth, 1, n3))


def _prep_kernel(ctx_ref, x_ref, mod_ref, h_ref, u_ref, *, n_ctx_tiles, d):
    is_ctx = pl.program_id(1) < n_ctx_tiles
    val = jnp.where(is_ctx, ctx_ref[...], x_ref[...])
    mod = mod_ref[...]
    mod = jnp.where(is_ctx, mod[0:1], mod)
    sh, sc = mod[:, :, 0:d], mod[:, :, d:2 * d]
    h_ref[...] = val
    u_ref[...] = (val * (1.0 + sc) + sh).astype(BF16)


def _mod_row_map(n_ctx_tiles, ctx_row):
    return lambda b, j: (jnp.where(j < n_ctx_tiles, ctx_row, b), 0, 0)


def _prep(x, ctx, mod0):
    bsz, t_len, d = x.shape
    c_len = ctx.shape[1]
    s_len = c_len + t_len
    nct, nt = c_len // ROW_TILE, s_len // ROW_TILE
    nb = PREP_BATCH
    assert bsz % nb == 0 and mod0.shape[0] >= bsz + nb
    kern = functools.partial(_prep_kernel, n_ctx_tiles=nct, d=d)
    tile = (nb, ROW_TILE, d)
    return pl.pallas_call(
        kern,
        grid=(bsz // nb, nt),
        in_specs=[
            pl.BlockSpec(tile, lambda b, j: (b, jnp.minimum(j, nct - 1), 0)),
            pl.BlockSpec(tile, lambda b, j: (b, jnp.maximum(j - nct, 0), 0)),
            pl.BlockSpec((nb, 1, 3 * d), _mod_row_map(nct, bsz // nb)),
        ],
        out_specs=[pl.BlockSpec(tile, lambda b, j: (b, j, 0)),
                   pl.BlockSpec(tile, lambda b, j: (b, j, 0))],
        out_shape=[jax.ShapeDtypeStruct((bsz, s_len, d), F32),
                   jax.ShapeDtypeStruct((bsz, s_len, d), BF16)],
        compiler_params=_params(2, 2 * nb * ROW_TILE * d * (4 + 4 + 4 + 2)),
        name="prep",
    )(ctx, x, mod0)


def _even_inproj_kernel(u_ref, w_ref, wg_ref, gb_ref, main_ref, g_ref):
    n = pl.program_id(1)
    u = u_ref[...]

    @pl.when(n == 0)
    def _():
        main_ref[...] = (_dot(u, w_ref[0]) * _silu(_dot(u, w_ref[1]))).astype(BF16)
        g_ref[...] = _dot(u, wg_ref[...]) + gb_ref[...]

    @pl.when(n == 1)
    def _():
        main_ref[...] = (_dot(u, w_ref[2]) * _dot(u, w_ref[3])).astype(BF16)

    @pl.when(n >= N_PAIRED)
    def _():
        main_ref[...] = _dot(u, w_ref[n + N_PAIRED]).astype(BF16)


def _even_inproj(u2d, w_main, w_gate, gate_b):
    rows, d = u2d.shape
    tm = INPROJ_ROWS
    n_out = N_PARTS - N_PAIRED
    vm = (2 * tm * d * 2 + N_PARTS * d * PART * 2 + 2 * tm * PART * 2
          + 3 * tm * PART * 4 + 2 * tm * GATE_PAD * 4 + 2 * d * GATE_PAD * 2)
    return pl.pallas_call(
        _even_inproj_kernel,
        grid=(rows // tm, n_out),
        in_specs=[
            pl.BlockSpec((tm, d), lambda i, n: (i, 0)),
            pl.BlockSpec((N_PARTS, d, PART), lambda i, n: (0, 0, 0),
                         pipeline_mode=pl.Buffered(1)),
            pl.BlockSpec((d, GATE_PAD), lambda i, n: (0, 0)),
            pl.BlockSpec((1, GATE_PAD), lambda i, n: (0, 0)),
        ],
        out_specs=[pl.BlockSpec((tm, PART), lambda i, n: (i, n)),
                   pl.BlockSpec((tm, GATE_PAD), lambda i, n: (i, 0))],
        out_shape=[jax.ShapeDtypeStruct((rows, n_out * PART), BF16),
                   jax.ShapeDtypeStruct((rows, GATE_PAD), F32)],
        compiler_params=_params(2, vm),
        name="even_inproj",
    )(u2d, w_main, w_gate, gate_b)


def _log_sigmoid(x):
    return jnp.minimum(x, 0.0) - jnp.log1p(jnp.exp(-jnp.abs(x)))


def _lane_scan(x, op, ident, reverse):
    n = x.shape[1]
    lane = lax.broadcasted_iota(jnp.int32, x.shape, 1)
    k = 1
    while k < n:
        if reverse:
            shifted = jnp.where(lane < n - k, pltpu.roll(x, n - k, 1), ident)
        else:
            shifted = jnp.where(lane >= k, pltpu.roll(x, k, 1), ident)
        x = op(x, shifted)
        k *= 2
    return x


def _gate_prep_kernel(g_ref, c_ref, r_ref, *, n_chunks, chunk):
    nh = MLSTM_HEADS
    assert 2 * nh == 8
    fwd, bwd = [], []
    for c in range(n_chunks):
        gt = g_ref[0, c * chunk:(c + 1) * chunk, :].T
        fwd.append(gt[0:8])
        bwd.append(gt[8:16])

    def scans(g, reverse):
        n_rows = g.shape[0]
        logf = pltpu.roll(_log_sigmoid(g) * LOG2_E, n_rows - nh, 0)
        b = _lane_scan(logf, jnp.add, 0.0, reverse)
        a = g * LOG2_E - b
        return b, a, _lane_scan(a, jnp.maximum, -jnp.inf, reverse)

    kinds = scans(jnp.concatenate(fwd, axis=0), False) + scans(jnp.concatenate(bwd, axis=0), True)
    sub = lax.broadcasted_iota(jnp.int32, (8, chunk), 0)
    pad = jnp.zeros((V7X_LANES - 8, chunk), F32)
    for c in range(n_chunks):
        for h in range(nh):
            row = c * 8 + h
            x8 = jnp.zeros((8, chunk), F32)
            for idx, arr in enumerate(kinds):
                x8 = jnp.where(sub == idx, arr[row:row + 1, :], x8)
            r_ref[0, h, c] = x8
            c_ref[0, h, c * chunk:(c + 1) * chunk, :] = jnp.concatenate([x8, pad], axis=0).T


def _gate_prep(gates):
    bsz, s_len, gp = gates.shape
    chunk = ROW_TILE
    nch = s_len // chunk
    nh = MLSTM_HEADS
    kern = functools.partial(_gate_prep_kernel, n_chunks=nch, chunk=chunk)
    vm = 2 * s_len * gp * 4 * (1 + nh) + 2 * nh * nch * 8 * chunk * 4
    return pl.pallas_call(
        kern,
        grid=(bsz,),
        in_specs=[pl.BlockSpec((1, s_len, gp), lambda b: (b, 0, 0))],
        out_specs=[pl.BlockSpec((1, nh, s_len, V7X_LANES), lambda b: (b, 0, 0, 0)),
                   pl.BlockSpec((1, nh, nch, 8, chunk), lambda b: (b, 0, 0, 0, 0))],
        out_shape=[jax.ShapeDtypeStruct((bsz, nh, s_len, V7X_LANES), F32),
                   jax.ShapeDtypeStruct((bsz, nh, nch, 8, chunk), F32)],
        compiler_params=_params(1, vm),
        name="gate_prep",
    )(gates)


def _mlstm_chunk(q, k, v, cols, a_row, m_state, c_prev, n_prev, direction, mask):
    chunk = q.shape[0]
    lanes = V7X_LANES
    last = chunk - 1 if direction == 0 else 0
    col0 = 3 * direction
    wide = lambda col: jnp.broadcast_to(col, (chunk, lanes))
    twice = lambda x: jnp.concatenate([x, x], axis=1)
    b_w = wide(cols[:, col0:col0 + 1])
    a_w = wide(cols[:, col0 + 1:col0 + 2])
    m_w = jnp.maximum(m_state, wide(cols[:, col0 + 2:col0 + 3]))
    ones = jnp.ones((chunk, lanes), BF16)

    w = jnp.exp2(jnp.where(mask, a_row - twice(m_w), -jnp.inf))
    a_inter = jnp.exp2(m_state - m_w)
    s = (_dot_nt(q, k) * w).astype(BF16)
    num = _dot(s, v) + twice(a_inter) * _dot(q, c_prev.astype(BF16))
    den = _dot(s, ones) + a_inter * _dot(q, n_prev.astype(BF16))
    h = num * twice(1.0 / jnp.maximum(jnp.abs(den), jnp.exp2(-(b_w + m_w))))

    m_last = m_w[last:last + 1, 0:1]
    decay = jnp.exp2(m_state - m_last)
    w_tok = jnp.exp2(a_w - m_last)
    k_t = k.astype(F32).T.astype(BF16)
    c_new = decay * c_prev + _dot(k_t, (v.astype(F32) * twice(w_tok)).astype(BF16))
    n_new = decay * n_prev + _dot(k_t, w_tok.astype(BF16))
    return h, c_new, n_new, b_w[last:last + 1, 0:1] + m_last


def _mlstm_kernel(q_ref, k_ref, v_ref, o_ref, z_ref, c_ref, r_ref, out_ref,
                  hf_ref, hb_ref, cst_ref, nst_ref, *, n_chunks, n_ctx_chunks, chunk, heads):
    dh = MLSTM_HEAD_DIM
    t_idx = lax.broadcasted_iota(jnp.int32, (chunk, chunk), 0)
    s_idx = lax.broadcasted_iota(jnp.int32, (chunk, chunk), 1)
    masks = (s_idx <= t_idx, s_idx >= t_idx)
    h_refs = (hf_ref, hb_ref)
    cst_ref[...] = jnp.zeros_like(cst_ref)
    nst_ref[...] = jnp.zeros_like(nst_ref)

    def body(jj, m_states):
        ci_bwd = jnp.where(jj < n_ctx_chunks, n_ctx_chunks - 1 - jj,
                           n_chunks - 1 - (jj - n_ctx_chunks))
        new_states = []
        for g in range(heads):
            lanes = slice(g * dh, (g + 1) * dh)
            for direction in range(2):
                ci = jj if direction == 0 else ci_bwd
                rows = pl.ds(pl.multiple_of(ci * chunk, chunk), chunk)
                idx = 2 * g + direction
                q = q_ref[0, rows, lanes]
                k = k_ref[0, rows, lanes] * jnp.asarray(MLSTM_K_SCALE, BF16)
                v = v_ref[0, rows, lanes]
                a_row = r_ref[0, g, ci][3 * direction + 1:3 * direction + 2, :]
                h, c_new, n_new, m_new = _mlstm_chunk(
                    q, k, v, c_ref[0, g, rows, :], a_row, m_states[idx],
                    cst_ref[idx], nst_ref[idx], direction, masks[direction])
                cst_ref[idx] = c_new
                nst_ref[idx] = n_new
                h_refs[direction][rows, lanes] = h
                new_states.append(m_new)
        return tuple(new_states)

    init = tuple(jnp.full((1, 1), NEG_INIT, F32) for _ in range(2 * heads))
    lax.fori_loop(0, n_chunks, body, init)

    def gate_body(ci, carry):
        rows = pl.ds(pl.multiple_of(ci * chunk, chunk), chunk)
        og = o_ref[0, rows, :].astype(F32)
        zg = z_ref[0, rows, :].astype(F32)
        out_ref[0, rows, :] = (_sigmoid(og) * (hf_ref[rows, :] + hb_ref[rows, :])
                               * _silu(zg)).astype(BF16)
        return carry

    lax.fori_loop(0, n_chunks, gate_body, 0)


def _mlstm(main, gate_cols, gate_rows, n_ctx_chunks):
    bsz, s_len, _ = main.shape
    nh, dh = MLSTM_HEADS, MLSTM_HEAD_DIM
    heads = MLSTM_HEADS_PER_STEP
    chunk = ROW_TILE
    assert chunk == dh
    nch = s_len // chunk
    width = heads * dh
    blocks_per_part = PART // width

    def part_spec(part):
        return pl.BlockSpec((1, s_len, width), lambda b, h: (b, 0, part * blocks_per_part + h))

    kern = functools.partial(_mlstm_kernel, n_chunks=nch, n_ctx_chunks=n_ctx_chunks,
                             chunk=chunk, heads=heads)
    vm = (2 * 6 * s_len * width * 2 + 2 * heads * s_len * V7X_LANES * 4
          + 2 * heads * nch * 8 * chunk * 4 + 2 * s_len * width * 4
          + 2 * heads * (dh + 1) * dh * 4 + 16 * chunk * chunk * 4)
    return pl.pallas_call(
        kern,
        grid=(bsz, nh // heads),
        in_specs=[part_spec(MLSTM_Q_PART + i) for i in range(5)] + [
                  pl.BlockSpec((1, heads, s_len, V7X_LANES), lambda b, h: (b, h, 0, 0)),
                  pl.BlockSpec((1, heads, nch, 8, chunk), lambda b, h: (b, h, 0, 0, 0))],
        out_specs=pl.BlockSpec((1, s_len, width), lambda b, h: (b, 0, h)),
        out_shape=jax.ShapeDtypeStruct((bsz, s_len, nh * dh), BF16),
        scratch_shapes=[pltpu.VMEM((s_len, width), F32),
                        pltpu.VMEM((s_len, width), F32),
                        pltpu.VMEM((2 * heads, dh, dh), F32),
                        pltpu.VMEM((2 * heads, dh, V7X_LANES), F32)],
        compiler_params=_params(2, vm),
        name="mlstm",
    )(main, main, main, main, main, gate_cols, gate_rows)


def _conv_mix(g_ref, p_ref, p_prev_ref, p_next_ref, w_ref, j, n_ctx_tiles, n_tiles):
    starts = jnp.logical_or(j == 0, j == n_ctx_tiles)
    ends = jnp.logical_or(j == n_ctx_tiles - 1, j == n_tiles - 1)
    p = p_ref[0].astype(F32)
    tile = p.shape[0]
    hb = V7X_BF16_SUBLANES
    prev = p_prev_ref[0].astype(F32)[hb - 1:hb, :] * jnp.where(starts, 0.0, 1.0)
    nxt = p_next_ref[0].astype(F32)[0:1, :] * jnp.where(ends, 0.0, 1.0)
    row = lax.broadcasted_iota(jnp.int32, p.shape, 0)
    p_dn = jnp.where(row == 0, prev, pltpu.roll(p, 1, 0))
    p_up = jnp.where(row == tile - 1, nxt, pltpu.roll(p, tile - 1, 0))
    w = w_ref[...]
    conv = w[0:1, :] * p_dn + w[1:2, :] * p + w[2:3, :] * p_up
    return (g_ref[0].astype(F32) * conv).astype(BF16)


def _outproj_kernel(*refs, d, has_next, conv, row_off, n_ctx_tiles, n_tiles):
    refs = list(refs)
    if conv:
        a1 = _conv_mix(*refs[:5], pl.program_id(1) + row_off, n_ctx_tiles, n_tiles)
        refs = refs[5:]
    else:
        a1 = refs[0][0]
        refs = refs[1:]
    if has_next:
        a2_ref, w_ref, h_ref, modc_ref, lng_ref, lnb_ref, modn_ref, hn_ref, u_ref = refs
    else:
        a2_ref, w_ref, h_ref, modc_ref, lng_ref, lnb_ref, hn_ref = refs
    y = _dot(a1, w_ref[0]) + _dot(a2_ref[0], w_ref[1])
    gate = modc_ref[0][:, 2 * d:3 * d]
    r = DEEPNORM_ALPHA * h_ref[0] + gate * y
    mu = jnp.mean(r, axis=-1, keepdims=True)
    xc = r - mu
    var = jnp.mean(xc * xc, axis=-1, keepdims=True)
    hn = xc * lax.rsqrt(var + LN_EPS) * lng_ref[...] + lnb_ref[...]
    hn_ref[0] = hn
    if has_next:
        modn = modn_ref[0]
        u_ref[0] = (hn * (1.0 + modn[:, d:2 * d]) + modn[:, 0:d]).astype(BF16)


def _outproj(a1_arr, a1_blk, a2_arr, a2_blk, w2, h, mod_cur, mod_next, ln_g, ln_b, n_ctx_tiles,
             conv_w=None, a_row_shift=0):
    bsz, s_len, d = h.shape
    tile = ROW_TILE
    nt = s_len // tile
    has_next = mod_next is not None
    off = 0 if has_next else n_ctx_tiles
    mod_map = _mod_row_map(n_ctx_tiles - off, bsz)
    row_blk = lambda b, j: (b, j + off, 0)

    def part_spec(part, shift=0):
        return pl.BlockSpec((1, tile, PART), lambda b, j: (b, j + off + shift, part))

    if conv_w is not None:
        assert a_row_shift == 0
        hb = V7X_BF16_SUBLANES
        per = tile // hb
        n_halo = s_len // hb
        in_specs = [
            part_spec(CONV_G_PART), part_spec(CONV_P_PART),
            pl.BlockSpec((1, hb, PART),
                         lambda b, j: (b, jnp.maximum((j + off) * per - 1, 0), CONV_P_PART)),
            pl.BlockSpec((1, hb, PART),
                         lambda b, j: (b, jnp.minimum((j + off + 1) * per, n_halo - 1), CONV_P_PART)),
            pl.BlockSpec(conv_w.shape, lambda b, j: (0, 0)),
        ]
        args = [a1_arr, a1_arr, a1_arr, a1_arr, conv_w]
    else:
        in_specs = [part_spec(a1_blk, a_row_shift)]
        args = [a1_arr]
    in_specs += [
        part_spec(a2_blk, a_row_shift),
        pl.BlockSpec(w2.shape, lambda b, j: (0, 0, 0)),
        pl.BlockSpec((1, tile, d), row_blk),
        pl.BlockSpec((1, 1, 3 * d), mod_map),
        pl.BlockSpec((1, d), lambda b, j: (0, 0)),
        pl.BlockSpec((1, d), lambda b, j: (0, 0)),
    ]
    args += [a2_arr, w2, h, mod_cur, ln_g, ln_b]
    out_specs = [pl.BlockSpec((1, tile, d), lambda b, j: (b, j, 0))]
    out_shape = [jax.ShapeDtypeStruct((bsz, s_len - off * tile, d), F32)]
    if has_next:
        in_specs.append(pl.BlockSpec((1, 1, 3 * d), mod_map))
        args.append(mod_next)
        out_specs.append(pl.BlockSpec((1, tile, d), lambda b, j: (b, j, 0)))
        out_shape.append(jax.ShapeDtypeStruct((bsz, s_len, d), BF16))
    kern = functools.partial(_outproj_kernel, d=d, has_next=has_next, conv=conv_w is not None,
                             row_off=off, n_ctx_tiles=n_ctx_tiles, n_tiles=nt)
    vm = (2 * (3 * tile * PART * 2 + 2 * PART * d * 2 + tile * d * (4 + 4 + 2))
          + 10 * tile * d * 4)
    res = pl.pallas_call(
        kern,
        grid=(bsz, nt - off),
        in_specs=in_specs,
        out_specs=out_specs,
        out_shape=out_shape,
        compiler_params=_params(2, vm),
        name="outproj_norm",
    )(*args)
    return res if has_next else res[0]


def _rope_pair(x, cs):
    prod = x * cs
    return prod + pltpu.roll(prod, QK_ROPE_DIM, 1)


def _mla_inproj_kernel(u_ref, win_ref, qn_ref, kvn_ref, wuq_ref, wukv_ref, cs_ref,
                       qp_ref, kp_ref, va_ref, z_ref):
    ql, kvl, lanes = Q_LORA_RANK, KV_LORA_RANK, V7X_LANES
    t = _dot(u_ref[0], win_ref[...])
    q_c = t[:, 0:ql]
    kv_c = t[:, ql:ql + kvl]
    kr2 = t[:, ql + kvl:ql + kvl + lanes]
    z_ref[0] = t[:, ql + kvl + lanes:].astype(BF16)
    cs = cs_ref[...]

    lane = lax.broadcasted_iota(jnp.int32, kr2.shape, 1)
    krt = jnp.where(lane < QK_ROPE_DIM, _rope_pair(kr2, cs), 0.0).astype(BF16)
    ones = jnp.ones(kr2.shape, BF16)

    qn = q_c * lax.rsqrt(jnp.mean(q_c * q_c, axis=-1, keepdims=True) + RMS_EPS) * qn_ref[...]
    q = _dot(qn.astype(BF16), wuq_ref[...])
    q_scale = MLA_SCALE * LOG2_E
    kvn = kv_c * lax.rsqrt(jnp.mean(kv_c * kv_c, axis=-1, keepdims=True) + RMS_EPS) * kvn_ref[...]
    kv = _dot(kvn.astype(BF16), wukv_ref[...])
    half = MLA_HEADS * QK_NOPE_DIM
    for h in range(MLA_HEADS):
        base = h * Q_HEAD_PAD
        qp_ref[0, :, base:base + lanes] = (q[:, base:base + lanes] * q_scale).astype(BF16)
        qp_ref[0, :, base + lanes:base + 2 * lanes] = (
            _rope_pair(q[:, base + lanes:base + 2 * lanes], cs) * q_scale).astype(BF16)
        kp_ref[0, :, base:base + lanes] = kv[:, h * lanes:(h + 1) * lanes].astype(BF16)
        kp_ref[0, :, base + lanes:base + 2 * lanes] = krt
        va_ref[0, :, base:base + lanes] = kv[:, half + h * lanes:half + (h + 1) * lanes].astype(BF16)
        va_ref[0, :, base + lanes:base + 2 * lanes] = ones


def _mla_inproj(u, w_in, q_norm, kv_norm, w_uq, w_ukv, cs):
    bsz, s_len, d = u.shape
    tile = ROW_TILE
    nt = s_len // tile
    n_in = w_in.shape[1]
    nq = w_uq.shape[1]
    nkv = w_ukv.shape[1]
    zdim = MLA_DIM
    full = lambda a: pl.BlockSpec(a.shape, lambda b, j: (0,) * a.ndim)
    row = lambda w: pl.BlockSpec((1, tile, w), lambda b, j: (b, j, 0))
    vm = (2 * (tile * d * 2 + d * n_in * 2 + w_uq.size * 2 + w_ukv.size * 2
               + tile * (3 * nq + zdim) * 2)
          + tile * (n_in + nq + nkv) * 4 * 2)
    return pl.pallas_call(
        _mla_inproj_kernel,
        grid=(bsz, nt),
        in_specs=[row(d), full(w_in), full(q_norm), full(kv_norm), full(w_uq), full(w_ukv),
                  pl.BlockSpec((tile, V7X_LANES), lambda b, j: (j, 0))],
        out_specs=[row(nq), row(nq), row(nq), row(zdim)],
        out_shape=[jax.ShapeDtypeStruct((bsz, s_len, nq), BF16),
                   jax.ShapeDtypeStruct((bsz, s_len, nq), BF16),
                   jax.ShapeDtypeStruct((bsz, s_len, nq), BF16),
                   jax.ShapeDtypeStruct((bsz, s_len, zdim), BF16)],
        compiler_params=_params(2, vm),
        name="mla_inproj",
    )(u, w_in, q_norm, kv_norm, w_uq, w_ukv, cs)


def _mla_attn_kernel(qa_ref, qb_ref, kp_ref, va_ref, za_ref, zb_ref, o_ref,
                     *, step_off, pad_tiles, n_ctx_steps, c_len, s_len, heads, tile):
    jg = pl.program_id(2) + step_off
    lanes = V7X_LANES
    q_refs, z_refs = (qa_ref, qb_ref), (za_ref, zb_ref)

    def scores(unit, n_keys):
        t, g = unit
        blk = slice(g * Q_HEAD_PAD, (g + 1) * Q_HEAD_PAD)
        return _dot_nt(q_refs[t][0, :, blk], kp_ref[0, 0:n_keys, blk])

    def softmax(s):
        return jnp.exp2(s - jnp.max(s, axis=1, keepdims=True)).astype(BF16)

    def values(unit, p, n_keys):
        t, g = unit
        oa = _dot(p, va_ref[0, 0:n_keys, g * Q_HEAD_PAD:(g + 1) * Q_HEAD_PAD])
        o = oa[:, 0:lanes] / oa[:, lanes:2 * lanes]
        zg = z_refs[t][0, :, g * lanes:(g + 1) * lanes].astype(F32)
        o_ref[0, t * tile:(t + 1) * tile, g * lanes:(g + 1) * lanes] = (o * _silu(zg)).astype(BF16)

    def attend(tiles, n_keys):
        units = [(t, g) for t in tiles for g in range(heads)]
        n = len(units)
        s = {0: scores(units[0], n_keys)}
        p = {}
        for i in range(n + 1):
            if i + 1 < n:
                s[i + 1] = scores(units[i + 1], n_keys)
            if i < n:
                p[i] = softmax(s.pop(i))
            if i >= 1:
                values(units[i - 1], p.pop(i - 1), n_keys)

    if step_off < n_ctx_steps:
        if pad_tiles:
            @pl.when(jg == 0)
            def _():
                attend((1,), c_len)
        if n_ctx_steps > pad_tiles:
            @pl.when(jnp.logical_and(jg >= pad_tiles, jg < n_ctx_steps))
            def _():
                attend((0, 1), c_len)

        @pl.when(jg >= n_ctx_steps)
        def _():
            attend((0, 1), s_len)
    else:
        attend((0, 1), s_len)


def _attn_pad_tiles(n_ctx_tiles):
    return n_ctx_tiles % 2


def _mla_attention(qp, kp, va, z, n_ctx_tiles, need_ctx):
    bsz, s_len, _ = qp.shape
    tile = ROW_TILE
    nt = s_len // tile
    pad = _attn_pad_tiles(n_ctx_tiles)
    assert (nt - n_ctx_tiles) % 2 == 0
    n_steps = (pad + nt) // 2
    n_ctx_steps = (pad + n_ctx_tiles) // 2
    step_off = 0 if need_ctx else n_ctx_steps
    heads = ATTN_HEADS_PER_STEP
    kern = functools.partial(_mla_attn_kernel, step_off=step_off, pad_tiles=pad,
                             n_ctx_steps=n_ctx_steps, c_len=n_ctx_tiles * tile, s_len=s_len,
                             heads=heads, tile=tile)
    lanes = V7X_LANES
    wide = heads * Q_HEAD_PAD
    tile_a = lambda j: jnp.maximum(2 * (j + step_off) - pad, 0)
    tile_b = lambda j: 2 * (j + step_off) + 1 - pad
    vm = (2 * (2 * tile * wide * 2 + 2 * s_len * wide * 2 + 4 * tile * heads * lanes * 2)
          + 3 * tile * s_len * (4 + 2))
    return pl.pallas_call(
        kern,
        grid=(bsz, MLA_HEADS // heads, n_steps - step_off),
        in_specs=[
            pl.BlockSpec((1, tile, wide), lambda b, h, j: (b, tile_a(j), h)),
            pl.BlockSpec((1, tile, wide), lambda b, h, j: (b, tile_b(j), h)),
            pl.BlockSpec((1, s_len, wide), lambda b, h, j: (b, 0, h)),
            pl.BlockSpec((1, s_len, wide), lambda b, h, j: (b, 0, h)),
            pl.BlockSpec((1, tile, heads * lanes), lambda b, h, j: (b, tile_a(j), h)),
            pl.BlockSpec((1, tile, heads * lanes), lambda b, h, j: (b, tile_b(j), h)),
        ],
        out_specs=pl.BlockSpec((1, 2 * tile, heads * lanes), lambda b, h, j: (b, j + step_off, h)),
        out_shape=jax.ShapeDtypeStruct((bsz, (pad + nt) * tile, MLA_DIM), BF16),
        compiler_params=_params(3, vm),
        name="mla_attention",
    )(qp, qp, kp, va, z, z)


def _even_weights(w_in, gate_b, w_out):
    d = w_in.shape[0]
    n_main = N_PARTS * PART
    ng = w_in.shape[1] - n_main
    w_main = w_in[:, :n_main].reshape(d, N_PARTS, PART).transpose(1, 0, 2).astype(BF16)
    w_main = w_main[jnp.array(EVEN_PART_ORDER)]
    w_gate = jnp.pad(w_in[:, n_main:], ((0, 0), (0, GATE_PAD - ng))).astype(BF16)
    gb = jnp.pad(gate_b, (0, GATE_PAD - ng)).reshape(1, GATE_PAD).astype(F32)
    w_out2 = w_out.reshape(2, PART, w_out.shape[1]).astype(BF16)
    return w_main, w_gate, gb, w_out2


def _swap_halves(w):
    half = w.shape[-1] // 2
    return jnp.concatenate([w[..., half:], w[..., :half]], axis=-1)


def _odd_weights(w_in, w_uq, w_ukv, w_out):
    ql, kvl, rope = Q_LORA_RANK, KV_LORA_RANK, QK_ROPE_DIM
    w_kr = w_in[:, ql + kvl:ql + kvl + rope]
    w_in_x = jnp.concatenate([w_in[:, :ql + kvl], w_kr, _swap_halves(w_kr),
                              w_in[:, ql + kvl + rope:]], axis=1).astype(BF16)
    wq = w_uq.reshape(ql, MLA_HEADS, QK_NOPE_DIM + rope)
    wq_r = wq[:, :, QK_NOPE_DIM:]
    w_uq_x = jnp.concatenate([wq[:, :, :QK_NOPE_DIM], wq_r, _swap_halves(wq_r)], axis=2)
    w_uq_x = w_uq_x.reshape(ql, MLA_HEADS * Q_HEAD_PAD).astype(BF16)
    wkv = w_ukv.reshape(kvl, MLA_HEADS, QK_NOPE_DIM + V_HEAD_DIM)
    w_ukv_x = jnp.concatenate([wkv[:, :, :QK_NOPE_DIM].reshape(kvl, -1),
                               wkv[:, :, QK_NOPE_DIM:].reshape(kvl, -1)], axis=1).astype(BF16)
    w_out2 = w_out.reshape(2, PART, w_out.shape[1]).astype(BF16)
    return w_in_x, w_uq_x, w_ukv_x, w_out2


def _rope_table(t_len, c_len):
    rows = t_len // GRID_W
    row = jnp.repeat(jnp.arange(rows, dtype=F32), GRID_W)
    col = jnp.tile(jnp.arange(GRID_W, dtype=F32), rows)
    n_freq = QK_ROPE_DIM // 4
    inv_freq = ROPE_BASE ** (-jnp.arange(n_freq, dtype=F32) / n_freq)
    ang = jnp.concatenate([row[:, None] * inv_freq, col[:, None] * inv_freq], axis=-1)
    cos, sin = jnp.cos(ang), jnp.sin(ang)
    lat = jnp.concatenate([cos, cos, -sin, sin], axis=-1)
    half = QK_ROPE_DIM
    ctx = jnp.concatenate([jnp.ones((c_len, half), F32), jnp.zeros((c_len, half), F32)], axis=-1)
    return jnp.concatenate([ctx, lat], axis=0)


def kernel(x, c, ctx, c_ctx, mod_w, mod_b, ln_g, ln_b, ev_w_in, ev_conv_w, ev_gate_b, ev_w_out,
           od_w_in, od_q_norm, od_w_uq, od_kv_norm, od_w_ukv, od_w_out):
    bsz, t_len, d = x.shape
    c_len = ctx.shape[1]
    s_len = c_len + t_len
    assert d == PART and c_len % ROW_TILE == 0 and t_len % ROW_TILE == 0
    assert (bsz * s_len) % INPROJ_ROWS == 0 and t_len % GRID_W == 0
    nct = c_len // ROW_TILE

    n_rows = -(-(bsz + 1) // MOD_ROWS_PAD) * MOD_ROWS_PAD
    cc = jnp.concatenate([c, c_ctx[None, :], jnp.zeros((n_rows - bsz - 1, d), F32)], axis=0)
    mods = _modulation(cc, mod_w, mod_b).reshape(DEPTH, n_rows, 1, 3 * d)

    cs = _rope_table(t_len, c_len)
    h, u = _prep(x, ctx, mods[0])

    for layer in range(DEPTH):
        j = layer // 2
        last = layer == DEPTH - 1
        mod_next = None if last else mods[layer + 1]
        lng, lnb = ln_g[layer].reshape(1, d), ln_b[layer].reshape(1, d)
        if layer % 2 == 0:
            w_main, w_gate, gb, w_out2 = _even_weights(ev_w_in[j], ev_gate_b[j], ev_w_out[j])
            main, gates = _even_inproj(u.reshape(bsz * s_len, d), w_main, w_gate, gb)
            main = main.reshape(bsz, s_len, (N_PARTS - N_PAIRED) * PART)
            gate_cols, gate_rows = _gate_prep(gates.reshape(bsz, s_len, GATE_PAD))
            mix_b = _mlstm(main, gate_cols, gate_rows, nct)
            res = _outproj(main, None, mix_b, 0, w_out2, h, mods[layer], mod_next, lng, lnb, nct,
                           conv_w=ev_conv_w[j])
        else:
            w_in_x, w_uq_x, w_ukv_x, w_out2 = _odd_weights(od_w_in[j], od_w_uq[j], od_w_ukv[j],
                                                           od_w_out[j])
            qp, kp, va, z = _mla_inproj(u, w_in_x, od_q_norm[j].reshape(1, -1),
                                        od_kv_norm[j].reshape(1, -1), w_uq_x, w_ukv_x, cs)
            og = _mla_attention(qp, kp, va, z, nct, need_ctx=not last)
            res = _outproj(og, 0, og, 1, w_out2, h, mods[layer], mod_next, lng, lnb, nct,
                           a_row_shift=_attn_pad_tiles(nct))
        if last:
            return res
        h, u = res
```
